```python
import jax, jax.numpy as jnp
from jax import lax
import numpy as np

D_MODEL = 1024
BATCH = 32
SEQ = 2048
DEPTH = 1

RET_HEADS = 4
RET_HEAD_DIM = 128
RET_WIDTH = RET_HEADS * RET_HEAD_DIM
RET_CHUNK = 128
RET_ROPE_BASE = 10000.0
ATT_HEADS = 8
ATT_HEAD_DIM = 64
ATT_WIDTH = ATT_HEADS * ATT_HEAD_DIM
ATT_ROT_DIM = ATT_HEAD_DIM // 4
ATT_ROPE_BASE = 500000.0
MOBA_BLOCK = 256
MOBA_TOPK = 3
QUERY_BLOCK = 128
MIX_WIDTH = RET_WIDTH + ATT_WIDTH
IN_WIDTH = 4 * RET_WIDTH + 3 * ATT_WIDTH
FFN_DIM = 2816
NORM_EPS = 1e-6

kernel_name = "hybrid_retention_moba_macaron"


def _rms(x, w=None):
    xf = x.astype(jnp.float32)
    y = xf * lax.rsqrt(jnp.mean(xf * xf, axis=-1, keepdims=True) + NORM_EPS)
    if w is not None:
        y = y * w.astype(jnp.float32)
    return y.astype(x.dtype)


def _rope(x, theta, rot_dim):
    s = x.shape[1]
    half = rot_dim // 2
    inv_freq = theta ** (-jnp.arange(half, dtype=jnp.float32) * 2.0 / rot_dim)
    ang = jnp.arange(s, dtype=jnp.float32)[:, None] * inv_freq[None, :]
    cos = jnp.cos(ang)[None, :, None, :]
    sin = jnp.sin(ang)[None, :, None, :]
    xf = x.astype(jnp.float32)
    x1 = xf[..., :half]
    x2 = xf[..., half:rot_dim]
    out = jnp.concatenate([x1 * cos - x2 * sin, x2 * cos + x1 * sin, xf[..., rot_dim:]], axis=-1)
    return out.astype(x.dtype)


def _swiglu(h, w_gate, w_up, w_down):
    return (jax.nn.silu(h @ w_gate) * (h @ w_up)) @ w_down


def _retention(q, k, v):
    b, s, h, d = q.shape
    c = RET_CHUNK
    n = s // c
    log_g = jnp.log(1.0 - 2.0 ** (-5.0 - jnp.arange(h, dtype=jnp.float32)))
    i = jnp.arange(c, dtype=jnp.float32)
    diff = i[:, None] - i[None, :]
    inner_decay = jnp.where(diff[None] >= 0,
                            jnp.exp(jnp.maximum(diff, 0.0)[None] * log_g[:, None, None]), 0.0)
    q_decay = jnp.exp((i[None, :] + 1.0) * log_g[:, None])[None, :, :, None]
    k_decay = jnp.exp((c - 1.0 - i[None, :]) * log_g[:, None])[None, :, :, None]
    chunk_decay = jnp.exp(c * log_g)[None, :, None, None]

    def to_chunks(t):
        return t.reshape(b, n, c, h, d).transpose(1, 0, 3, 2, 4)

    def step(state, qkv):
        qc, kc, vc = qkv
        scores = jnp.einsum('bhid,bhjd->bhij', qc, kc) * inner_decay
        o = jnp.einsum('bhij,bhjd->bhid', scores, vc)
        o = o + jnp.einsum('bhid,bhde->bhie', qc, state) * q_decay
        state = state * chunk_decay + jnp.einsum('bhjd,bhje->bhde', kc * k_decay, vc)
        return state, o

    state0 = jnp.zeros((b, h, d, d), jnp.float32)
    _, o = lax.scan(step, state0, (to_chunks(q), to_chunks(k), to_chunks(v)))
    return o.transpose(1, 0, 3, 2, 4).reshape(b, s, h, d)


def _moba(q, k, v):
    b, s, h, dh = q.shape
    bs = MOBA_BLOCK
    qb = QUERY_BLOCK
    nb = -(-s // bs)
    pad = nb * bs - s
    nq = s // qb
    n_sel = min(MOBA_TOPK, nb)
    scale = dh ** -0.5
    qh = q.transpose(0, 2, 1, 3)
    kh = jnp.pad(k.transpose(0, 2, 1, 3), ((0, 0), (0, 0), (0, pad), (0, 0)))
    vh = jnp.pad(v.transpose(0, 2, 1, 3), ((0, 0), (0, 0), (0, pad), (0, 0)))
    k_blocks = kh.reshape(b, h, nb, bs, dh)
    v_blocks = vh.reshape(b, h, nb, bs, dh)
    k_mean = jnp.mean(k_blocks.astype(jnp.float32), axis=3)
    own = jnp.arange(s) // bs
    gate = jnp.einsum('bhsd,bhnd->bhsn', qh.astype(jnp.float32), k_mean)
    past = jnp.arange(nb)[None, :] < own[:, None]
    gate = jnp.where(past, gate, -jnp.inf)
    _, idx = lax.top_k(gate, n_sel)
    valid = idx < own[:, None]

    def blocks(t):
        t = t.reshape(b, h, nq, qb, *t.shape[3:])
        t = jnp.moveaxis(t, 2, 1)
        return t.reshape(b * nq, h, qb, *t.shape[4:])

    batch_id = jnp.repeat(jnp.arange(b), nq)
    qblk_id = jnp.tile(jnp.arange(nq), b)
    head_ix = jnp.arange(h)[:, None, None]

    def attend(args):
        qi, idx_i, valid_i, bi, ji = args
        kb = k_blocks[bi]
        vb = v_blocks[bi]
        k_sel = kb[head_ix, idx_i]
        v_sel = vb[head_ix, idx_i]
        own_blk = (ji * qb) // bs
        k_own = lax.dynamic_index_in_dim(kb, own_blk, axis=1, keepdims=False)
        v_own = lax.dynamic_index_in_dim(vb, own_blk, axis=1, keepdims=False)
        s_sel = jnp.einsum('hqd,hqnkd->hqnk', qi, k_sel).astype(jnp.float32) * scale
        s_sel = jnp.where(valid_i[..., None], s_sel, -jnp.inf)
        s_own = jnp.einsum('hqd,hkd->hqk', qi, k_own).astype(jnp.float32) * scale
        q_pos = ji * qb + jnp.arange(qb)
        k_pos = own_blk * bs + jnp.arange(bs)
        s_own = jnp.where(k_pos[None, None, :] <= q_pos[None, :, None], s_own, -jnp.inf)
        logits = jnp.concatenate([s_sel.reshape(h, qb, n_sel * bs), s_own], axis=-1)
        p = jax.nn.softmax(logits, axis=-1).astype(qi.dtype)
        p_sel = p[..., :n_sel * bs].reshape(h, qb, n_sel, bs)
        p_own = p[..., n_sel * bs:]
        return (jnp.einsum('hqnk,hqnkd->hqd', p_sel, v_sel)
                + jnp.einsum('hqk,hkd->hqd', p_own, v_own))

    out = lax.map(attend, (blocks(qh), blocks(idx), blocks(valid), batch_id, qblk_id))
    return out.reshape(b, nq, h, qb, dh).transpose(0, 1, 3, 2, 4).reshape(b, s, h * dh)


def setup_inputs(seed: int = 0) -> dict:
    key = jax.random.key(seed)
    ks = jax.random.split(key, 16)
    L = DEPTH

    def w(k, shape, fan_in):
        return jax.random.normal(k, shape, jnp.float32) * fan_in ** -0.5

    def gain(k, shape):
        return 1.0 + 0.02 * jax.random.normal(k, shape, jnp.float32)

    return {
        "x": jax.random.normal(ks[0], (BATCH, SEQ, D_MODEL), jnp.float32),
        "ffn1_norm_w": gain(ks[1], (L, D_MODEL)),
        "ffn1_w_gate": w(ks[2], (L, D_MODEL, FFN_DIM), D_MODEL),
        "ffn1_w_up": w(ks[3], (L, D_MODEL, FFN_DIM), D_MODEL),
        "ffn1_w_down": w(ks[4], (L, FFN_DIM, D_MODEL), FFN_DIM),
        "mix_norm_w": gain(ks[5], (L, D_MODEL)),
        "w_in": w(ks[6], (L, D_MODEL, IN_WIDTH), D_MODEL),
        "ret_out_beta": gain(ks[7], (L, RET_WIDTH)),
        "q_norm_w": gain(ks[8], (L, ATT_HEAD_DIM)),
        "k_norm_w": gain(ks[9], (L, ATT_HEAD_DIM)),
        "att_out_beta": gain(ks[10], (L, ATT_WIDTH)),
        "w_out": w(ks[11], (L, MIX_WIDTH, D_MODEL), MIX_WIDTH),
        "ffn2_norm_w": gain(ks[12], (L, D_MODEL)),
        "ffn2_w_gate": w(ks[13], (L, D_MODEL, FFN_DIM), D_MODEL),
        "ffn2_w_up": w(ks[14], (L, D_MODEL, FFN_DIM), D_MODEL),
        "ffn2_w_down": w(ks[15], (L, FFN_DIM, D_MODEL), FFN_DIM),
    }


def reference(x, ffn1_norm_w, ffn1_w_gate, ffn1_w_up, ffn1_w_down, mix_norm_w, w_in,
              ret_out_beta, q_norm_w, k_norm_w, att_out_beta, w_out,
              ffn2_norm_w, ffn2_w_gate, ffn2_w_up, ffn2_w_down):
    b, s, _ = x.shape
    R = RET_WIDTH
    A = ATT_WIDTH
    for l in range(DEPTH):
        x = x + 0.5 * _swiglu(_rms(x, ffn1_norm_w[l]), ffn1_w_gate[l], ffn1_w_up[l], ffn1_w_down[l])

        hn = _rms(x, mix_norm_w[l])
        proj = hn @ w_in[l]
        rq = proj[..., 0:R].reshape(b, s, RET_HEADS, RET_HEAD_DIM)
        rk = proj[..., R:2 * R].reshape(b, s, RET_HEADS, RET_HEAD_DIM)
        rv = proj[..., 2 * R:3 * R].reshape(b, s, RET_HEADS, RET_HEAD_DIM)
        rg = proj[..., 3 * R:4 * R]
        o0 = 4 * R
        aq = proj[..., o0:o0 + A].reshape(b, s, ATT_HEADS, ATT_HEAD_DIM)
        ak = proj[..., o0 + A:o0 + 2 * A].reshape(b, s, ATT_HEADS, ATT_HEAD_DIM)
        av = proj[..., o0 + 2 * A:o0 + 3 * A].reshape(b, s, ATT_HEADS, ATT_HEAD_DIM)

        rq = _rope(rq, RET_ROPE_BASE, RET_HEAD_DIM)
        rk = _rope(rk, RET_ROPE_BASE, RET_HEAD_DIM) * (RET_HEAD_DIM ** -0.5)
        ret = _retention(rq.astype(jnp.float32), rk.astype(jnp.float32), rv.astype(jnp.float32))
        ret = _rms(ret).reshape(b, s, R).astype(x.dtype)
        ret = ret * jax.nn.silu(rg) * ret_out_beta[l]

        aq = _rope(_rms(aq, q_norm_w[l]), ATT_ROPE_BASE, ATT_ROT_DIM)
        ak = _rope(_rms(ak, k_norm_w[l]), ATT_ROPE_BASE, ATT_ROT_DIM)
        att = _moba(aq, ak, av)
        att = _rms(att.reshape(b, s, ATT_HEADS, ATT_HEAD_DIM)).reshape(b, s, A) * att_out_beta[l]

        x = x + jnp.concatenate([ret, att], axis=-1) @ w_out[l]

        x = x + 0.5 * _swiglu(_rms(x, ffn2_norm_w[l]), ffn2_w_gate[l], ffn2_w_up[l], ffn2_w_down[l])
    return x
```

```python
import functools
import math

import jax
import jax.numpy as jnp
from jax import lax
from jax.experimental import pallas as pl
from jax.experimental.pallas import tpu as pltpu

D_MODEL = 1024
RET_HEADS = 4
RET_HEAD_DIM = 128
RET_WIDTH = RET_HEADS * RET_HEAD_DIM
RET_CHUNK = 128
RET_ROPE_BASE = 10000.0
ATT_HEADS = 8
ATT_HEAD_DIM = 64
ATT_WIDTH = ATT_HEADS * ATT_HEAD_DIM
ATT_ROT_DIM = ATT_HEAD_DIM // 4
ATT_ROPE_BASE = 500000.0
MOBA_BLOCK = 256
MOBA_TOPK = 3
FFN_DIM = 2816
NORM_EPS = 1e-6

V7X_LANES = 128
V7X_MXU_DIM = 256
V7X_VMEM_LIMIT_BYTES = 56 * 1024 * 1024

F32 = jnp.float32
BF16 = jnp.bfloat16
NEG_BIG = -1e30
HEADS_PER_LANE_TILE = V7X_LANES // ATT_HEAD_DIM
ATT_LANE_TILES = ATT_WIDTH // V7X_LANES


def _tiles(seq):
    row_tile = 512 if seq % 512 == 0 else 256
    assert seq % row_tile == 0 and seq % MOBA_BLOCK == 0 and seq % RET_CHUNK == 0
    return dict(row_tile=row_tile, q_tile=MOBA_BLOCK)


def _params():
    return pltpu.CompilerParams(vmem_limit_bytes=V7X_VMEM_LIMIT_BYTES)


def _resident(shape):
    zeros = (0,) * len(shape)
    return pl.BlockSpec(shape, lambda *_: zeros, pipeline_mode=pl.Buffered(1))


def _rms_rows(x):
    return x * lax.rsqrt(jnp.mean(x * x, axis=-1, keepdims=True) + NORM_EPS)


def _swiglu_half_step(x, norm_w, wg_ref, wu_ref, wd_ref):
    h = (_rms_rows(x) * norm_w).astype(BF16)
    g = jnp.dot(h, wg_ref[...], preferred_element_type=F32)
    u = jnp.dot(h, wu_ref[...], preferred_element_type=F32)
    a = (g * jax.nn.sigmoid(g) * u).astype(BF16)
    return x + 0.5 * jnp.dot(a, wd_ref[...], preferred_element_type=F32)


def _split_bf16(x):
    hi = x.astype(BF16)
    lo = (x - hi.astype(F32)).astype(BF16)
    return hi, lo


def _ffn_inproj_kernel(x_ref, n1_ref, wg_ref, wu_ref, wd_ref, n2_ref, win_ref,
                       rcos_ref, rsin_ref, acos_ref, asin_lo_ref, asin_hi_ref,
                       qg_ref, kg_ref, ones_ref,
                       x1_ref, rq_ref, rk_ref, rv_ref, rg_ref, aq_ref, ak_ref, av_ref):
    x1 = _swiglu_half_step(x_ref[...], n1_ref[...], wg_ref, wu_ref, wd_ref)
    x1_ref[...] = x1
    hn = (_rms_rows(x1) * n2_ref[...]).astype(BF16)

    def proj(col0, width):
        return jnp.dot(hn, win_ref[:, col0:col0 + width], preferred_element_type=F32)

    R, A, L = RET_WIDTH, ATT_WIDTH, V7X_LANES
    rcos, rsin = rcos_ref[...], rsin_ref[...]

    def ret_rope(p, out_ref, scale):
        for h in range(RET_HEADS):
            t = p[:, h * L:(h + 1) * L]
            r = t * rcos + pltpu.roll(t, RET_HEAD_DIM // 2, 1) * rsin
            if scale is not None:
                r = r * scale
            out_ref[:, h * L:(h + 1) * L] = r.astype(out_ref.dtype)

    ret_rope(proj(0, R), rq_ref, None)
    ret_rope(proj(R, R), rk_ref, RET_HEAD_DIM ** -0.5)
    rv_ref[...] = proj(2 * R, R).astype(rv_ref.dtype)
    rg_ref[...] = proj(3 * R, R).astype(rg_ref.dtype)

    acos, asin_lo, asin_hi = acos_ref[...], asin_lo_ref[...], asin_hi_ref[...]
    half_rot = ATT_ROT_DIM // 2

    def att_norm_rope(p, gain_ref, out_ref):
        for c in range(A // V7X_MXU_DIM):
            t = p[:, c * V7X_MXU_DIM:(c + 1) * V7X_MXU_DIM]
            hi, lo = _split_bf16(t * t)
            ss = (jnp.dot(hi, ones_ref[...], preferred_element_type=F32)
                  + jnp.dot(lo, ones_ref[...], preferred_element_type=F32))
            tn = t * lax.rsqrt(ss * (1.0 / ATT_HEAD_DIM) + NORM_EPS)
            tn = tn * gain_ref[:, c * V7X_MXU_DIM:(c + 1) * V7X_MXU_DIM]
            for s in range(V7X_MXU_DIM // L):
                u = tn[:, s * L:(s + 1) * L]
                r = (u * acos + pltpu.roll(u, half_rot, 1) * asin_hi
                     + pltpu.roll(u, L - half_rot, 1) * asin_lo)
                col = c * V7X_MXU_DIM + s * L
                out_ref[:, col:col + L] = r.astype(out_ref.dtype)

    o0 = 4 * R
    att_norm_rope(proj(o0, A), qg_ref, aq_ref)
    att_norm_rope(proj(o0 + A, A), kg_ref, ak_ref)
    av_ref[...] = proj(o0 + 2 * A, A).astype(av_ref.dtype)


def _ffn_inproj(x2d, seq, p, tables):
    tokens = x2d.shape[0]
    tm = _tiles(seq)["row_tile"]
    pos_blocks = seq // tm
    row = lambda w: pl.BlockSpec((tm, w), lambda i: (i, 0))
    pos = lambda: pl.BlockSpec((tm, V7X_LANES), lambda i: (i % pos_blocks, 0))
    in_width = p["w_in"].shape[1]
    out_shapes = ([jax.ShapeDtypeStruct((tokens, D_MODEL), F32)]
                  + [jax.ShapeDtypeStruct((tokens, RET_WIDTH), BF16)] * 3
                  + [jax.ShapeDtypeStruct((tokens, RET_WIDTH), F32)]
                  + [jax.ShapeDtypeStruct((tokens, ATT_WIDTH), BF16)] * 3)
    return pl.pallas_call(
        _ffn_inproj_kernel,
        grid=(tokens // tm,),
        in_specs=[row(D_MODEL), _resident((1, D_MODEL)),
                  _resident((D_MODEL, FFN_DIM)), _resident((D_MODEL, FFN_DIM)),
                  _resident((FFN_DIM, D_MODEL)), _resident((1, D_MODEL)),
                  _resident((D_MODEL, in_width)),
                  pos(), pos(), pos(), pos(), pos(),
                  _resident((1, ATT_WIDTH)), _resident((1, ATT_WIDTH)),
                  _resident((V7X_MXU_DIM, V7X_MXU_DIM))],
        out_specs=[row(D_MODEL)] + [row(RET_WIDTH)] * 4 + [row(ATT_WIDTH)] * 3,
        out_shape=out_shapes,
        compiler_params=_params(),
        name="ffn1_inproj",
    )(x2d, p["n1"], p["wg1"], p["wu1"], p["wd1"], p["n2"], p["w_in"],
      tables["rcos"], tables["rsin"], tables["acos"], tables["asin_lo"], tables["asin_hi"],
      p["qg"], p["kg"], tables["ones_bd"])


def _retention_kernel(q_ref, k_ref, v_ref, g_ref, beta_ref, inner_ref, qdec_ref, kdec_ref, cdec_ref,
                      o_ref):
    c = RET_CHUNK
    n_chunks = q_ref.shape[1] // c
    inner, qdec, kdec, cdec = inner_ref[0], qdec_ref[0], kdec_ref[0], cdec_ref[0]
    beta = beta_ref[...]

    def step(i, state):
        rows = pl.ds(pl.multiple_of(i * c, c), c)
        qc, kc, vc = q_ref[0, rows, :], k_ref[0, rows, :], v_ref[0, rows, :]
        scores = lax.dot_general(qc, kc, (((1,), (1,)), ((), ())), preferred_element_type=F32) * inner
        o = jnp.dot(scores.astype(BF16), vc, preferred_element_type=F32)
        o = o + jnp.dot(qc, state.astype(BF16), preferred_element_type=F32) * qdec
        kd = (kc.astype(F32) * kdec).astype(BF16)
        state = state * cdec + lax.dot_general(kd, vc, (((0,), (0,)), ((), ())),
                                               preferred_element_type=F32)
        g = g_ref[0, rows, :]
        o = _rms_rows(o) * (g * jax.nn.sigmoid(g)) * beta
        o_ref[0, rows, :] = o.astype(o_ref.dtype)
        return state

    lax.fori_loop(0, n_chunks, step, jnp.zeros((RET_HEAD_DIM, RET_HEAD_DIM), F32))


def _retention(rq, rk, rv, rg, beta, tables):
    b, s, _ = rq.shape
    d = RET_HEAD_DIM
    head = lambda: pl.BlockSpec((1, s, d), lambda bi, hi: (bi, 0, hi))
    dec = lambda: pl.BlockSpec((1, RET_CHUNK, d), lambda bi, hi: (hi, 0, 0))
    return pl.pallas_call(
        _retention_kernel,
        grid=(b, RET_HEADS),
        in_specs=[head(), head(), head(), head(),
                  pl.BlockSpec((1, d), lambda bi, hi: (0, hi)),
                  dec(), dec(), dec(), pl.BlockSpec((1, 1, d), lambda bi, hi: (hi, 0, 0))],
        out_specs=head(),
        out_shape=jax.ShapeDtypeStruct((b, s, RET_WIDTH), BF16),
        compiler_params=_params(),
        name="retention",
    )(rq, rk, rv, rg, beta, tables["inner_decay"], tables["q_decay"], tables["k_decay"],
      tables["chunk_decay"])


def _moba_kernel(q_ref, k_ref, v_ref, beta_ref, o_ref, vaug_ref, kmean_ref, sel_ref):
    bs = MOBA_BLOCK
    tq = q_ref.shape[1]
    seq = k_ref.shape[1]
    nb = seq // bs
    j = pl.program_id(2)
    L = V7X_LANES

    @pl.when(j == 0)
    def _():
        vaug_ref[:, :L] = v_ref[0]
        vaug_ref[:, L:] = jnp.ones((seq, L), BF16)
        km = jnp.mean(k_ref[0].astype(F32).reshape(nb, bs, L), axis=1)
        kmean_ref[...] = jnp.zeros(kmean_ref.shape, F32)
        kmean_ref[:nb, :] = km

    lane = lax.broadcasted_iota(jnp.int32, (tq, L), 1)
    qs = q_ref[0] * (ATT_HEAD_DIM ** -0.5)
    km_hi, km_lo = _split_bf16(kmean_ref[...])
    nt = (((1,), (1,)), ((), ()))
    row_id = lax.broadcasted_iota(jnp.int32, (kmean_ref.shape[0], tq), 0)
    q_pos = lax.broadcasted_iota(jnp.int32, (tq, bs), 0)
    k_pos = lax.broadcasted_iota(jnp.int32, (tq, bs), 1)
    causal = k_pos <= q_pos

    def block_update(carry, qh, n, sel):
        m_run, l_run, o_run = carry
        rows = pl.ds(pl.multiple_of(n * bs, bs), bs)
        s = lax.dot_general(qh, k_ref[0, rows, :], nt, preferred_element_type=F32)
        if sel is None:
            s = jnp.where(causal, s, -jnp.inf)
        m_blk = jnp.broadcast_to(jnp.max(s, axis=-1, keepdims=True), (tq, L))
        p = jnp.exp(s - m_blk[:, :1]).astype(BF16)
        ov = jnp.dot(p, vaug_ref[rows, :], preferred_element_type=F32)
        if sel is None:
            m_new = jnp.maximum(m_run, m_blk)
            w_blk = jnp.exp(m_blk - m_new)
        else:
            m_new = jnp.maximum(m_run, jnp.where(sel, m_blk, NEG_BIG))
            w_blk = jnp.where(sel, jnp.exp(m_blk - m_new), 0.0)
        w_run = jnp.exp(m_run - m_new)
        return (m_new, w_run * l_run + w_blk * ov[:, L:], w_run * o_run + w_blk * ov[:, :L])

    heads = []
    for h in range(HEADS_PER_LANE_TILE):
        in_head = (lane >= h * ATT_HEAD_DIM) & (lane < (h + 1) * ATT_HEAD_DIM)
        qh = jnp.where(in_head, qs, jnp.zeros_like(qs))
        gate = (lax.dot_general(km_hi, qh, nt, preferred_element_type=F32)
                + lax.dot_general(km_lo, qh, nt, preferred_element_type=F32))
        gate = jnp.where(row_id < j, gate, -jnp.inf)
        rank = jnp.zeros(gate.shape, jnp.int32)
        for m in range(nb):
            other = gate[m:m + 1, :]
            ahead = (other > gate) | ((other == gate) & (m < row_id))
            rank = rank + ahead.astype(jnp.int32)
        sel_ref[...] = ((row_id < j) & (rank < MOBA_TOPK)).astype(F32)

        def past(n, carry, qh=qh):
            sel_row = sel_ref[pl.ds(n, 1), :]
            sel = jnp.transpose(jnp.broadcast_to(sel_row, (L, tq))) > 0.5
            return block_update(carry, qh, n, sel)

        init = (jnp.full((tq, L), NEG_BIG, F32), jnp.zeros((tq, L), F32), jnp.zeros((tq, L), F32))
        carry = lax.fori_loop(0, j, past, init)
        _, l_fin, o_fin = block_update(carry, qh, j, None)
        heads.append((in_head, o_fin / l_fin))

    att = jnp.zeros((tq, L), F32)
    for in_head, res in heads:
        ms = jnp.sum(jnp.where(in_head, res * res, 0.0), axis=-1, keepdims=True) * (1.0 / ATT_HEAD_DIM)
        att = jnp.where(in_head, res * lax.rsqrt(ms + NORM_EPS), att)
    o_ref[0] = (att * beta_ref[...]).astype(o_ref.dtype)


def _moba(aq, ak, av, beta):
    b, s, _ = aq.shape
    tq = _tiles(s)["q_tile"]
    L = V7X_LANES
    kmean_rows = 16
    assert s // MOBA_BLOCK <= kmean_rows and tq == MOBA_BLOCK
    kv = lambda: pl.BlockSpec((1, s, L), lambda bi, hp, j: (bi, 0, hp))
    qo = lambda: pl.BlockSpec((1, tq, L), lambda bi, hp, j: (bi, j, hp))
    return pl.pallas_call(
        _moba_kernel,
        grid=(b, ATT_LANE_TILES, s // tq),
        in_specs=[qo(), kv(), kv(), pl.BlockSpec((1, L), lambda bi, hp, j: (0, hp))],
        out_specs=qo(),
        out_shape=jax.ShapeDtypeStruct((b, s, ATT_WIDTH), BF16),
        scratch_shapes=[pltpu.VMEM((s, 2 * L), BF16), pltpu.VMEM((kmean_rows, L), F32),
                        pltpu.VMEM((kmean_rows, tq), F32)],
        compiler_params=pltpu.CompilerParams(
            vmem_limit_bytes=V7X_VMEM_LIMIT_BYTES,
            dimension_semantics=("arbitrary", "arbitrary", "arbitrary")),
        name="moba",
    )(aq, ak, av, beta)


def _outproj_ffn_kernel(x1_ref, ret_ref, att_ref, wo_ret_ref, wo_att_ref, n_ref, wg_ref, wu_ref, wd_ref,
                        o_ref):
    x2 = (x1_ref[...]
          + jnp.dot(ret_ref[...], wo_ret_ref[...], preferred_element_type=F32)
          + jnp.dot(att_ref[...], wo_att_ref[...], preferred_element_type=F32))
    o_ref[...] = _swiglu_half_step(x2, n_ref[...], wg_ref, wu_ref, wd_ref)


def _outproj_ffn(x1, ret, att, seq, p):
    tokens = x1.shape[0]
    tm = _tiles(seq)["row_tile"]
    row = lambda w: pl.BlockSpec((tm, w), lambda i: (i, 0))
    return pl.pallas_call(
        _outproj_ffn_kernel,
        grid=(tokens // tm,),
        in_specs=[row(D_MODEL), row(RET_WIDTH), row(ATT_WIDTH),
                  _resident((RET_WIDTH, D_MODEL)), _resident((ATT_WIDTH, D_MODEL)),
                  _resident((1, D_MODEL)),
                  _resident((D_MODEL, FFN_DIM)), _resident((D_MODEL, FFN_DIM)),
                  _resident((FFN_DIM, D_MODEL))],
        out_specs=row(D_MODEL),
        out_shape=jax.ShapeDtypeStruct((tokens, D_MODEL), F32),
        compiler_params=_params(),
        name="outproj_ffn2",
    )(x1, ret, att, p["wo_ret"], p["wo_att"], p["n3"], p["wg2"], p["wu2"], p["wd2"])


def _tables(seq):
    L = V7X_LANES
    pos = jnp.arange(seq, dtype=F32)[:, None]

    def angles(theta, rot_dim):
        half = rot_dim // 2
        inv_freq = theta ** (-jnp.arange(half, dtype=F32) * 2.0 / rot_dim)
        return pos * inv_freq[None, :]

    ang = angles(RET_ROPE_BASE, RET_HEAD_DIM)
    rcos = jnp.concatenate([jnp.cos(ang), jnp.cos(ang)], axis=1)
    rsin = jnp.concatenate([-jnp.sin(ang), jnp.sin(ang)], axis=1)

    ang = angles(ATT_ROPE_BASE, ATT_ROT_DIM)
    half = ATT_ROT_DIM // 2
    pad = jnp.zeros((seq, ATT_HEAD_DIM - ATT_ROT_DIM), F32)
    zero_half = jnp.zeros((seq, half), F32)
    head_cos = jnp.concatenate([jnp.cos(ang), jnp.cos(ang), 1.0 + pad], axis=1)
    head_sin_lo = jnp.concatenate([-jnp.sin(ang), zero_half, pad], axis=1)
    head_sin_hi = jnp.concatenate([zero_half, jnp.sin(ang), pad], axis=1)
    rep = lambda t: jnp.tile(t, (1, HEADS_PER_LANE_TILE))

    c = RET_CHUNK
    log_g = jnp.log(1.0 - 2.0 ** (-5.0 - jnp.arange(RET_HEADS, dtype=F32)))
    i = jnp.arange(c, dtype=F32)
    diff = i[:, None] - i[None, :]
    inner = jnp.where(diff[None] >= 0,
                      jnp.exp(jnp.maximum(diff, 0.0)[None] * log_g[:, None, None]), 0.0)
    lanes = lambda t: jnp.broadcast_to(t[:, :, None], (RET_HEADS, t.shape[1], RET_HEAD_DIM))
    q_decay = lanes(jnp.exp((i[None, :] + 1.0) * log_g[:, None]))
    k_decay = lanes(jnp.exp((c - 1.0 - i[None, :]) * log_g[:, None]))
    chunk_decay = lanes(jnp.exp(c * log_g)[:, None])

    group = jnp.arange(V7X_MXU_DIM) // ATT_HEAD_DIM
    ones_bd = (group[:, None] == group[None, :]).astype(BF16)
    return dict(rcos=rcos, rsin=rsin, acos=rep(head_cos), asin_lo=rep(head_sin_lo),
                asin_hi=rep(head_sin_hi), inner_decay=inner, q_decay=q_decay, k_decay=k_decay,
                chunk_decay=chunk_decay, ones_bd=ones_bd)


def kernel(x, ffn1_norm_w, ffn1_w_gate, ffn1_w_up, ffn1_w_down, mix_norm_w, w_in, ret_out_beta,
           q_norm_w, k_norm_w, att_out_beta, w_out, ffn2_norm_w, ffn2_w_gate, ffn2_w_up, ffn2_w_down):
    b, s, d = x.shape
    assert d == D_MODEL
    tables = _tables(s)
    depth = ffn1_norm_w.shape[0]
    x2d = x.reshape(b * s, d)
    for l in range(depth):
        p = dict(
            n1=ffn1_norm_w[l][None, :], wg1=ffn1_w_gate[l].astype(BF16), wu1=ffn1_w_up[l].astype(BF16),
            wd1=ffn1_w_down[l].astype(BF16), n2=mix_norm_w[l][None, :], w_in=w_in[l].astype(BF16),
            qg=jnp.tile(q_norm_w[l], ATT_HEADS)[None, :], kg=jnp.tile(k_norm_w[l], ATT_HEADS)[None, :],
            wo_ret=w_out[l][:RET_WIDTH].astype(BF16), wo_att=w_out[l][RET_WIDTH:].astype(BF16),
            n3=ffn2_norm_w[l][None, :], wg2=ffn2_w_gate[l].astype(BF16), wu2=ffn2_w_up[l].astype(BF16),
            wd2=ffn2_w_down[l].astype(BF16))
        x1, rq, rk, rv, rg, aq, ak, av = _ffn_inproj(x2d, s, p, tables)
        seq3 = lambda t: t.reshape(b, s, t.shape[-1])
        ret = _retention(seq3(rq), seq3(rk), seq3(rv), seq3(rg), ret_out_beta[l][None, :], tables)
        att = _moba(seq3(aq), seq3(ak), seq3(av), att_out_beta[l][None, :])
        x2d = _outproj_ffn(x1, ret.reshape(b * s, RET_WIDTH), att.reshape(b * s, ATT_WIDTH), s, p)
    return x2d.reshape(b, s, d)
```

```python
import functools
import math

import jax
import jax.numpy as jnp
from jax import lax
from jax.experimental import pallas as pl
from jax.experimental.pallas import tpu as pltpu

D_MODEL = 1024
RET_HEADS = 4
RET_HEAD_DIM = 128
RET_WIDTH = RET_HEADS * RET_HEAD_DIM
RET_CHUNK = 128
RET_ROPE_BASE = 10000.0
ATT_HEADS = 8
ATT_HEAD_DIM = 64
ATT_WIDTH = ATT_HEADS * ATT_HEAD_DIM
ATT_ROT_DIM = ATT_HEAD_DIM // 4
ATT_ROPE_BASE = 500000.0
MOBA_BLOCK = 256
MOBA_TOPK = 3
FFN_DIM = 2816
NORM_EPS = 1e-6

V7X_LANES = 128
V7X_MXU_DIM = 256
V7X_VMEM_LIMIT_BYTES = 56 * 1024 * 1024

F32 = jnp.float32
BF16 = jnp.bfloat16
NEG_BIG = -1e30
HEADS_PER_LANE_TILE = V7X_LANES // ATT_HEAD_DIM
ATT_LANE_TILES = ATT_WIDTH // V7X_LANES


def _tiles(seq):
    row_tile = 512 if seq % 512 == 0 else 256
    assert seq % row_tile == 0 and seq % MOBA_BLOCK == 0 and seq % RET_CHUNK == 0
    return dict(row_tile=row_tile, q_tile=MOBA_BLOCK)


def _params():
    return pltpu.CompilerParams(vmem_limit_bytes=V7X_VMEM_LIMIT_BYTES)


def _resident(shape):
    zeros = (0,) * len(shape)
    return pl.BlockSpec(shape, lambda *_: zeros, pipeline_mode=pl.Buffered(1))


def _rms_rows(x):
    return x * lax.rsqrt(jnp.mean(x * x, axis=-1, keepdims=True) + NORM_EPS)


def _swiglu_half_step(x, norm_w, wg_ref, wu_ref, wd_ref):
    h = (_rms_rows(x) * norm_w).astype(BF16)
    g = jnp.dot(h, wg_ref[...], preferred_element_type=F32)
    u = jnp.dot(h, wu_ref[...], preferred_element_type=F32)
    a = (g * jax.nn.sigmoid(g) * u).astype(BF16)
    return x + 0.5 * jnp.dot(a, wd_ref[...], preferred_element_type=F32)


def _split_bf16(x):
    hi = x.astype(BF16)
    lo = (x - hi.astype(F32)).astype(BF16)
    return hi, lo


def _ffn_inproj_kernel(x_ref, n1_ref, wg_ref, wu_ref, wd_ref, n2_ref, win_ref,
                       rcos_ref, rsin_ref, acos_ref, asin_lo_ref, asin_hi_ref,
                       qg_ref, kg_ref, ones_ref,
                       x1_ref, rq_ref, rk_ref, rv_ref, rg_ref, aq_ref, ak_ref, av_ref):
    x1 = _swiglu_half_step(x_ref[...], n1_ref[...], wg_ref, wu_ref, wd_ref)
    x1_ref[...] = x1
    hn = (_rms_rows(x1) * n2_ref[...]).astype(BF16)

    def proj(col0, width):
        return jnp.dot(hn, win_ref[:, col0:col0 + width], preferred_element_type=F32)

    R, A, L = RET_WIDTH, ATT_WIDTH, V7X_LANES
    rcos, rsin = rcos_ref[...], rsin_ref[...]

    def ret_rope(p, out_ref, scale):
        for h in range(RET_HEADS):
            t = p[:, h * L:(h + 1) * L]
            r = t * rcos + pltpu.roll(t, RET_HEAD_DIM // 2, 1) * rsin
            if scale is not None:
                r = r * scale
            out_ref[:, h * L:(h + 1) * L] = r.astype(out_ref.dtype)

    ret_rope(proj(0, R), rq_ref, None)
    ret_rope(proj(R, R), rk_ref, RET_HEAD_DIM ** -0.5)
    rv_ref[...] = proj(2 * R, R).astype(rv_ref.dtype)
    rg_ref[...] = proj(3 * R, R).astype(rg_ref.dtype)

    acos, asin_lo, asin_hi = acos_ref[...], asin_lo_ref[...], asin_hi_ref[...]
    half_rot = ATT_ROT_DIM // 2

    def att_norm_rope(p, gain_ref, out_ref):
        for c in range(A // V7X_MXU_DIM):
            t = p[:, c * V7X_MXU_DIM:(c + 1) * V7X_MXU_DIM]
            hi, lo = _split_bf16(t * t)
            ss = (jnp.dot(hi, ones_ref[...], preferred_element_type=F32)
                  + jnp.dot(lo, ones_ref[...], preferred_element_type=F32))
            tn = t * lax.rsqrt(ss * (1.0 / ATT_HEAD_DIM) + NORM_EPS)
            tn = tn * gain_ref[:, c * V7X_MXU_DIM:(c + 1) * V7X_MXU_DIM]
            for s in range(V7X_MXU_DIM // L):
                u = tn[:, s * L:(s + 1) * L]
                r = (u * acos + pltpu.roll(u, half_rot, 1) * asin_hi
                     + pltpu.roll(u, L - half_rot, 1) * asin_lo)
                col = c * V7X_MXU_DIM + s * L
                out_ref[:, col:col + L] = r.astype(out_ref.dtype)

    o0 = 4 * R
    att_norm_rope(proj(o0, A), qg_ref, aq_ref)
    att_norm_rope(proj(o0 + A, A), kg_ref, ak_ref)
    av_ref[...] = proj(o0 + 2 * A, A).astype(av_ref.dtype)


def _ffn_inproj(x2d, seq, p, tables):
    tokens = x2d.shape[0]
    tm = _tiles(seq)["row_tile"]
    pos_blocks = seq // tm
    row = lambda w: pl.BlockSpec((tm, w), lambda i: (i, 0))
    pos = lambda: pl.BlockSpec((tm, V7X_LANES), lambda i: (i % pos_blocks, 0))
    in_width = p["w_in"].shape[1]
    out_shapes = ([jax.ShapeDtypeStruct((tokens, D_MODEL), F32)]
                  + [jax.ShapeDtypeStruct((tokens, RET_WIDTH), BF16)] * 3
                  + [jax.ShapeDtypeStruct((tokens, RET_WIDTH), F32)]
                  + [jax.ShapeDtypeStruct((tokens, ATT_WIDTH), BF16)] * 3)
    return pl.pallas_call(
        _ffn_inproj_kernel,
        grid=(tokens // tm,),
        in_specs=[row(D_MODEL), _resident((1, D_MODEL)),
                  _resident((D_MODEL, FFN_DIM)), _resident((D_MODEL, FFN_DIM)),
                  _resident((FFN_DIM, D_MODEL)), _resident((1, D_MODEL)),
                  _resident((D_MODEL, in_width)),
                  pos(), pos(), pos(), pos(), pos(),
                  _resident((1, ATT_WIDTH)), _resident((1, ATT_WIDTH)),
                  _resident((V7X_MXU_DIM, V7X_MXU_DIM))],
        out_specs=[row(D_MODEL)] + [row(RET_WIDTH)] * 4 + [row(ATT_WIDTH)] * 3,
        out_shape=out_shapes,
        compiler_params=_params(),
        name="ffn1_inproj",
    )(x2d, p["n1"], p["wg1"], p["wu1"], p["wd1"], p["n2"], p["w_in"],
      tables["rcos"], tables["rsin"], tables["acos"], tables["asin_lo"], tables["asin_hi"],
      p["qg"], p["kg"], tables["ones_bd"])


def _retention_kernel(q_ref, k_ref, v_ref, g_ref, beta_ref, inner_ref, qdec_ref, kdec_ref, cdec_ref,
                      o_ref):
    c = RET_CHUNK
    n_chunks = q_ref.shape[1] // c
    inner, qdec, kdec, cdec = inner_ref[0], qdec_ref[0], kdec_ref[0], cdec_ref[0]
    beta = beta_ref[...]

    def step(i, state):
        rows = pl.ds(pl.multiple_of(i * c, c), c)
        qc, kc, vc = q_ref[0, rows, :], k_ref[0, rows, :], v_ref[0, rows, :]
        scores = lax.dot_general(qc, kc, (((1,), (1,)), ((), ())), preferred_element_type=F32) * inner
        o = jnp.dot(scores.astype(BF16), vc, preferred_element_type=F32)
        o = o + jnp.dot(qc, state.astype(BF16), preferred_element_type=F32) * qdec
        kd = (kc.astype(F32) * kdec).astype(BF16)
        state = state * cdec + lax.dot_general(kd, vc, (((0,), (0,)), ((), ())),
                                               preferred_element_type=F32)
        g = g_ref[0, rows, :]
        o = _rms_rows(o) * (g * jax.nn.sigmoid(g)) * beta
        o_ref[0, rows, :] = o.astype(o_ref.dtype)
        return state

    lax.fori_loop(0, n_chunks, step, jnp.zeros((RET_HEAD_DIM, RET_HEAD_DIM), F32))


def _retention(rq, rk, rv, rg, beta, tables):
    b, s, _ = rq.shape
    d = RET_HEAD_DIM
    head = lambda: pl.BlockSpec((1, s, d), lambda bi, hi: (bi, 0, hi))
    dec = lambda: pl.BlockSpec((1, RET_CHUNK, d), lambda bi, hi: (hi, 0, 0))
    return pl.pallas_call(
        _retention_kernel,
        grid=(b, RET_HEADS),
        in_specs=[head(), head(), head(), head(),
                  pl.BlockSpec((1, d), lambda bi, hi: (0, hi)),
                  dec(), dec(), dec(), pl.BlockSpec((1, 1, d), lambda bi, hi: (hi, 0, 0))],
        out_specs=head(),
        out_shape=jax.ShapeDtypeStruct((b, s, RET_WIDTH), BF16),
        compiler_params=_params(),
        name="retention",
    )(rq, rk, rv, rg, beta, tables["inner_decay"], tables["q_decay"], tables["k_decay"],
      tables["chunk_decay"])


ONES_ROWS = 16
STAT_ROWS = 8


def _moba_kernel(q_ref, k_ref, v_ref, beta_ref, o_ref,
                 vt_ref, kmean_ref, sel_ref, m_ref, l_ref, acc_ref):
    bs = MOBA_BLOCK
    tq = q_ref.shape[1]
    seq = k_ref.shape[1]
    nb = seq // bs
    dh = ATT_HEAD_DIM
    L = V7X_LANES
    j = pl.program_id(2)
    nt = (((1,), (1,)), ((), ()))

    @pl.when(j == 0)
    def _():
        vt = jnp.transpose(v_ref[0].astype(F32))
        for h in range(HEADS_PER_LANE_TILE):
            for n in range(nb):
                vt_ref[h, n, :dh, :] = vt[h * dh:(h + 1) * dh, n * bs:(n + 1) * bs].astype(BF16)
                vt_ref[h, n, dh:, :] = jnp.ones((ONES_ROWS, bs), BF16)
        kmean_ref[...] = jnp.zeros(kmean_ref.shape, F32)
        kmean_ref[:nb, :] = jnp.mean(k_ref[0].astype(F32).reshape(nb, bs, L), axis=1)

    m_ref[...] = jnp.full(m_ref.shape, NEG_BIG, F32)
    l_ref[...] = jnp.zeros(l_ref.shape, F32)
    acc_ref[...] = jnp.zeros(acc_ref.shape, F32)

    lane = lax.broadcasted_iota(jnp.int32, (tq, L), 1)
    qs = q_ref[0] * (dh ** -0.5)
    qh = [jnp.where((lane >= h * dh) & (lane < (h + 1) * dh), qs, jnp.zeros_like(qs))
          for h in range(HEADS_PER_LANE_TILE)]

    km_hi, km_lo = _split_bf16(kmean_ref[...])
    row_id = lax.broadcasted_iota(jnp.int32, (kmean_ref.shape[0], tq), 0)
    for h in range(HEADS_PER_LANE_TILE):
        gate = (lax.dot_general(km_hi, qh[h], nt, preferred_element_type=F32)
                + lax.dot_general(km_lo, qh[h], nt, preferred_element_type=F32))
        gate = jnp.where(row_id < j, gate, -jnp.inf)
        rank = jnp.zeros(gate.shape, jnp.int32)
        for m in range(nb):
            other = gate[m:m + 1, :]
            ahead = (other > gate) | ((other == gate) & (m < row_id))
            rank = rank + ahead.astype(jnp.int32)
        sel_ref[h] = ((row_id < j) & (rank < MOBA_TOPK)).astype(F32)

    causal_t = (lax.broadcasted_iota(jnp.int32, (bs, tq), 0)
                <= lax.broadcasted_iota(jnp.int32, (bs, tq), 1))

    def attend(blocks):
        chains = [(h, n, own) for n, own in blocks for h in range(HEADS_PER_LANE_TILE)]
        scores = []
        for h, n, own in chains:
            kn = k_ref[0, pl.ds(pl.multiple_of(n * bs, bs), bs), :]
            scores.append(lax.dot_general(kn, qh[h], nt, preferred_element_type=F32))
        probs = []
        for (h, n, own), st in zip(chains, scores):
            if own:
                st = jnp.where(causal_t, st, -jnp.inf)
            mt = jnp.max(st, axis=0, keepdims=True)
            probs.append((mt, jnp.exp(st - mt).astype(BF16)))
        partial = []
        for (h, n, own), (mt, pt) in zip(chains, probs):
            ov = jnp.dot(vt_ref[h, n], pt, preferred_element_type=F32)
            partial.append((h, n, own, mt, ov))
        for h, n, own, mt, ov in partial:
            m_blk = jnp.broadcast_to(mt, (STAT_ROWS, tq))
            m_run = m_ref[h]
            if own:
                m_new = jnp.maximum(m_run, m_blk)
                w_blk = jnp.exp(m_blk - m_new)
            else:
                sel = jnp.broadcast_to(sel_ref[h, pl.ds(n, 1), :], (STAT_ROWS, tq)) > 0.5
                m_new = jnp.maximum(m_run, jnp.where(sel, m_blk, NEG_BIG))
                w_blk = jnp.where(sel, jnp.exp(m_blk - m_new), 0.0)
            w_run = jnp.exp(m_run - m_new)
            m_ref[h] = m_new
            l_ref[h] = w_run * l_ref[h] + w_blk * ov[dh:dh + STAT_ROWS, :]
            acc_ref[h] = (jnp.broadcast_to(w_run[:1], (dh, tq)) * acc_ref[h]
                          + jnp.broadcast_to(w_blk[:1], (dh, tq)) * ov[:dh, :])

    def pair(i, carry):
        attend([(2 * i, False), (2 * i + 1, False)])
        return carry

    lax.fori_loop(0, lax.shift_right_logical(j, 1), pair, 0)

    @pl.when((j & 1) == 1)
    def _():
        attend([(j - 1, False), (j, True)])

    @pl.when((j & 1) == 0)
    def _():
        attend([(j, True)])

    normed = []
    for h in range(HEADS_PER_LANE_TILE):
        res = acc_ref[h] * jnp.broadcast_to((1.0 / l_ref[h])[:1], (dh, tq))
        ms = jnp.mean(res * res, axis=0, keepdims=True)
        normed.append(res * lax.rsqrt(ms + NORM_EPS))
    att = jnp.transpose(jnp.concatenate(normed, axis=0))
    o_ref[0] = (att * beta_ref[...]).astype(o_ref.dtype)


def _moba(aq, ak, av, beta):
    b, s, _ = aq.shape
    tq = _tiles(s)["q_tile"]
    L = V7X_LANES
    nb = s // MOBA_BLOCK
    kmean_rows = 16
    assert nb <= kmean_rows and tq == MOBA_BLOCK
    kv = lambda: pl.BlockSpec((1, s, L), lambda bi, hp, j: (bi, 0, hp))
    qo = lambda: pl.BlockSpec((1, tq, L), lambda bi, hp, j: (bi, j, hp))
    hpt = HEADS_PER_LANE_TILE
    return pl.pallas_call(
        _moba_kernel,
        grid=(b, ATT_LANE_TILES, s // tq),
        in_specs=[qo(), kv(), kv(), pl.BlockSpec((1, L), lambda bi, hp, j: (0, hp))],
        out_specs=qo(),
        out_shape=jax.ShapeDtypeStruct((b, s, ATT_WIDTH), BF16),
        scratch_shapes=[pltpu.VMEM((hpt, nb, ATT_HEAD_DIM + ONES_ROWS, MOBA_BLOCK), BF16),
                        pltpu.VMEM((kmean_rows, L), F32),
                        pltpu.VMEM((hpt, kmean_rows, tq), F32),
                        pltpu.VMEM((hpt, STAT_ROWS, tq), F32),
                        pltpu.VMEM((hpt, STAT_ROWS, tq), F32),
                        pltpu.VMEM((hpt, ATT_HEAD_DIM, tq), F32)],
        compiler_params=pltpu.CompilerParams(
            vmem_limit_bytes=V7X_VMEM_LIMIT_BYTES,
            dimension_semantics=("arbitrary", "arbitrary", "arbitrary")),
        name="moba",
    )(aq, ak, av, beta)


def _outproj_ffn_kernel(x1_ref, ret_ref, att_ref, wo_ret_ref, wo_att_ref, n_ref, wg_ref, wu_ref, wd_ref,
                        o_ref):
    x2 = (x1_ref[...]
          + jnp.dot(ret_ref[...], wo_ret_ref[...], preferred_element_type=F32)
          + jnp.dot(att_ref[...], wo_att_ref[...], preferred_element_type=F32))
    o_ref[...] = _swiglu_half_step(x2, n_ref[...], wg_ref, wu_ref, wd_ref)


def _outproj_ffn(x1, ret, att, seq, p):
    tokens = x1.shape[0]
    tm = _tiles(seq)["row_tile"]
    row = lambda w: pl.BlockSpec((tm, w), lambda i: (i, 0))
    return pl.pallas_call(
        _outproj_ffn_kernel,
        grid=(tokens // tm,),
        in_specs=[row(D_MODEL), row(RET_WIDTH), row(ATT_WIDTH),
                  _resident((RET_WIDTH, D_MODEL)), _resident((ATT_WIDTH, D_MODEL)),
                  _resident((1, D_MODEL)),
                  _resident((D_MODEL, FFN_DIM)), _resident((D_MODEL, FFN_DIM)),
                  _resident((FFN_DIM, D_MODEL))],
        out_specs=row(D_MODEL),
        out_shape=jax.ShapeDtypeStruct((tokens, D_MODEL), F32),
        compiler_params=_params(),
        name="outproj_ffn2",
    )(x1, ret, att, p["wo_ret"], p["wo_att"], p["n3"], p["wg2"], p["wu2"], p["wd2"])


def _tables(seq):
    L = V7X_LANES
    pos = jnp.arange(seq, dtype=F32)[:, None]

    def angles(theta, rot_dim):
        half = rot_dim // 2
        inv_freq = theta ** (-jnp.arange(half, dtype=F32) * 2.0 / rot_dim)
        return pos * inv_freq[None, :]

    ang = angles(RET_ROPE_BASE, RET_HEAD_DIM)
    rcos = jnp.concatenate([jnp.cos(ang), jnp.cos(ang)], axis=1)
    rsin = jnp.concatenate([-jnp.sin(ang), jnp.sin(ang)], axis=1)

    ang = angles(ATT_ROPE_BASE, ATT_ROT_DIM)
    half = ATT_ROT_DIM // 2
    pad = jnp.zeros((seq, ATT_HEAD_DIM - ATT_ROT_DIM), F32)
    zero_half = jnp.zeros((seq, half), F32)
    head_cos = jnp.concatenate([jnp.cos(ang), jnp.cos(ang), 1.0 + pad], axis=1)
    head_sin_lo = jnp.concatenate([-jnp.sin(ang), zero_half, pad], axis=1)
    head_sin_hi = jnp.concatenate([zero_half, jnp.sin(ang), pad], axis=1)
    rep = lambda t: jnp.tile(t, (1, HEADS_PER_LANE_TILE))

    c = RET_CHUNK
    log_g = jnp.log(1.0 - 2.0 ** (-5.0 - jnp.arange(RET_HEADS, dtype=F32)))
    i = jnp.arange(c, dtype=F32)
    diff = i[:, None] - i[None, :]
    inner = jnp.where(diff[None] >= 0,
                      jnp.exp(jnp.maximum(diff, 0.0)[None] * log_g[:, None, None]), 0.0)
    lanes = lambda t: jnp.broadcast_to(t[:, :, None], (RET_HEADS, t.shape[1], RET_HEAD_DIM))
    q_decay = lanes(jnp.exp((i[None, :] + 1.0) * log_g[:, None]))
    k_decay = lanes(jnp.exp((c - 1.0 - i[None, :]) * log_g[:, None]))
    chunk_decay = lanes(jnp.exp(c * log_g)[:, None])

    group = jnp.arange(V7X_MXU_DIM) // ATT_HEAD_DIM
    ones_bd = (group[:, None] == group[None, :]).astype(BF16)
    return dict(rcos=rcos, rsin=rsin, acos=rep(head_cos), asin_lo=rep(head_sin_lo),
                asin_hi=rep(head_sin_hi), inner_decay=inner, q_decay=q_decay, k_decay=k_decay,
                chunk_decay=chunk_decay, ones_bd=ones_bd)


def kernel(x, ffn1_norm_w, ffn1_w_gate, ffn1_w_up, ffn1_w_down, mix_norm_w, w_in, ret_out_beta,
           q_norm_w, k_norm_w, att_out_beta, w_out, ffn2_norm_w, ffn2_w_gate, ffn2_w_up, ffn2_w_down):
    b, s, d = x.shape
    assert d == D_MODEL
    tables = _tables(s)
    depth = ffn1_norm_w.shape[0]
    x2d = x.reshape(b * s, d)
    for l in range(depth):
        p = dict(
            n1=ffn1_norm_w[l][None, :], wg1=ffn1_w_gate[l].astype(BF16), wu1=ffn1_w_up[l].astype(BF16),
            wd1=ffn1_w_down[l].astype(BF16), n2=mix_norm_w[l][None, :], w_in=w_in[l].astype(BF16),
            qg=jnp.tile(q_norm_w[l], ATT_HEADS)[None, :], kg=jnp.tile(k_norm_w[l], ATT_HEADS)[None, :],
            wo_ret=w_out[l][:RET_WIDTH].astype(BF16), wo_att=w_out[l][RET_WIDTH:].astype(BF16),
            n3=ffn2_norm_w[l][None, :], wg2=ffn2_w_gate[l].astype(BF16), wu2=ffn2_w_up[l].astype(BF16),
            wd2=ffn2_w_down[l].astype(BF16))
        x1, rq, rk, rv, rg, aq, ak, av = _ffn_inproj(x2d, s, p, tables)
        seq3 = lambda t: t.reshape(b, s, t.shape[-1])
        ret = _retention(seq3(rq), seq3(rk), seq3(rv), seq3(rg), ret_out_beta[l][None, :], tables)
        att = _moba(seq3(aq), seq3(ak), seq3(av), att_out_beta[l][None, :])
        x2d = _outproj_ffn(x1, ret.reshape(b * s, RET_WIDTH), att.reshape(b * s, ATT_WIDTH), s, p)
    return x2d.reshape(b, s, d)
```

```python
import math

import jax
import jax.numpy as jnp
from jax import lax
from jax.experimental import pallas as pl
from jax.experimental.pallas import tpu as pltpu

D_MODEL = 1024
RET_HEADS = 4
RET_HEAD_DIM = 128
RET_WIDTH = RET_HEADS * RET_HEAD_DIM
RET_ROPE_BASE = 10000.0
ATT_HEADS = 8
ATT_HEAD_DIM = 64
ATT_WIDTH = ATT_HEADS * ATT_HEAD_DIM
ATT_ROT_DIM = ATT_HEAD_DIM // 4
ATT_ROPE_BASE = 500000.0
MOBA_BLOCK = 256
MOBA_TOPK = 3
FFN_DIM = 2816
NORM_EPS = 1e-6

V7X_LANES = 128
V7X_MXU_DIM = 256
V7X_VMEM_LIMIT_BYTES = 56 * 1024 * 1024

F32 = jnp.float32
BF16 = jnp.bfloat16
HEADS_PER_LANE_TILE = V7X_LANES // ATT_HEAD_DIM
ATT_LANE_TILES = ATT_WIDTH // V7X_LANES
RET_KERNEL_CHUNK = V7X_MXU_DIM

NT = (((1,), (1,)), ((), ()))
TN = (((0,), (0,)), ((), ()))


def _tiles(seq):
    row_tile = 512 if seq % 512 == 0 else 256
    assert seq % row_tile == 0 and seq % MOBA_BLOCK == 0 and seq % RET_KERNEL_CHUNK == 0
    return dict(row_tile=row_tile)


def _params():
    return pltpu.CompilerParams(vmem_limit_bytes=V7X_VMEM_LIMIT_BYTES)


def _resident(shape):
    zeros = (0,) * len(shape)
    return pl.BlockSpec(shape, lambda *_: zeros, pipeline_mode=pl.Buffered(1))


def _rms_rows(x):
    return x * lax.rsqrt(jnp.mean(x * x, axis=-1, keepdims=True) + NORM_EPS)


def _swiglu_half_step(x, norm_w, wg_ref, wu_ref, wd_ref):
    h = (_rms_rows(x) * norm_w).astype(BF16)
    g = jnp.dot(h, wg_ref[...], preferred_element_type=F32)
    u = jnp.dot(h, wu_ref[...], preferred_element_type=F32)
    a = (g * jax.nn.sigmoid(g) * u).astype(BF16)
    return x + 0.5 * jnp.dot(a, wd_ref[...], preferred_element_type=F32)


def _split_bf16(x):
    hi = x.astype(BF16)
    lo = (x - hi.astype(F32)).astype(BF16)
    return hi, lo


def _ffn_inproj_kernel(x_ref, n1_ref, wg_ref, wu_ref, wd_ref, n2_ref, win_ref,
                       rcos_ref, rsin_ref, acos_ref, asin_lo_ref, asin_hi_ref,
                       qg_ref, kg_ref, ones_ref,
                       x1_ref, rq_ref, rk_ref, rv_ref, rg_ref, aq_ref, ak_ref, av_ref):
    x1 = _swiglu_half_step(x_ref[...], n1_ref[...], wg_ref, wu_ref, wd_ref)
    x1_ref[...] = x1
    hn = (_rms_rows(x1) * n2_ref[...]).astype(BF16)

    def proj(col0, width):
        return jnp.dot(hn, win_ref[:, col0:col0 + width], preferred_element_type=F32)

    R, A, L = RET_WIDTH, ATT_WIDTH, V7X_LANES
    rcos, rsin = rcos_ref[...], rsin_ref[...]

    def ret_rope(p, out_ref, scale):
        for h in range(RET_HEADS):
            t = p[:, h * L:(h + 1) * L]
            r = t * rcos + pltpu.roll(t, RET_HEAD_DIM // 2, 1) * rsin
            if scale is not None:
                r = r * scale
            out_ref[:, h * L:(h + 1) * L] = r.astype(out_ref.dtype)

    ret_rope(proj(0, R), rq_ref, None)
    ret_rope(proj(R, R), rk_ref, RET_HEAD_DIM ** -0.5)
    rv_ref[...] = proj(2 * R, R).astype(rv_ref.dtype)
    rg_ref[...] = proj(3 * R, R).astype(rg_ref.dtype)

    acos, asin_lo, asin_hi = acos_ref[...], asin_lo_ref[...], asin_hi_ref[...]
    half_rot = ATT_ROT_DIM // 2

    def att_norm_rope(p, gain_ref, out_ref):
        for c in range(A // V7X_MXU_DIM):
            t = p[:, c * V7X_MXU_DIM:(c + 1) * V7X_MXU_DIM]
            hi, lo = _split_bf16(t * t)
            ss = (jnp.dot(hi, ones_ref[...], preferred_element_type=F32)
                  + jnp.dot(lo, ones_ref[...], preferred_element_type=F32))
            tn = t * lax.rsqrt(ss * (1.0 / ATT_HEAD_DIM) + NORM_EPS)
            tn = tn * gain_ref[:, c * V7X_MXU_DIM:(c + 1) * V7X_MXU_DIM]
            for s in range(V7X_MXU_DIM // L):
                u = tn[:, s * L:(s + 1) * L]
                r = (u * acos + pltpu.roll(u, half_rot, 1) * asin_hi
                     + pltpu.roll(u, L - half_rot, 1) * asin_lo)
                col = c * V7X_MXU_DIM + s * L
                out_ref[:, col:col + L] = r.astype(out_ref.dtype)

    o0 = 4 * R
    att_norm_rope(proj(o0, A), qg_ref, aq_ref)
    att_norm_rope(proj(o0 + A, A), kg_ref, ak_ref)
    av_ref[...] = proj(o0 + 2 * A, A).astype(av_ref.dtype)


def _ffn_inproj(x2d, seq, p, tables):
    tokens = x2d.shape[0]
    tm = _tiles(seq)["row_tile"]
    pos_blocks = seq // tm
    row = lambda w: pl.BlockSpec((tm, w), lambda i: (i, 0))
    pos = lambda: pl.BlockSpec((tm, V7X_LANES), lambda i: (i % pos_blocks, 0))
    in_width = p["w_in"].shape[1]
    out_shapes = ([jax.ShapeDtypeStruct((tokens, D_MODEL), F32)]
                  + [jax.ShapeDtypeStruct((tokens, RET_WIDTH), BF16)] * 3
                  + [jax.ShapeDtypeStruct((tokens, RET_WIDTH), F32)]
                  + [jax.ShapeDtypeStruct((tokens, ATT_WIDTH), BF16)] * 3)
    return pl.pallas_call(
        _ffn_inproj_kernel,
        grid=(tokens // tm,),
        in_specs=[row(D_MODEL), _resident((1, D_MODEL)),
                  _resident((D_MODEL, FFN_DIM)), _resident((D_MODEL, FFN_DIM)),
                  _resident((FFN_DIM, D_MODEL)), _resident((1, D_MODEL)),
                  _resident((D_MODEL, in_width)),
                  pos(), pos(), pos(), pos(), pos(),
                  _resident((1, ATT_WIDTH)), _resident((1, ATT_WIDTH)),
                  _resident((V7X_MXU_DIM, V7X_MXU_DIM))],
        out_specs=[row(D_MODEL)] + [row(RET_WIDTH)] * 4 + [row(ATT_WIDTH)] * 3,
        out_shape=out_shapes,
        compiler_params=_params(),
        name="ffn1_inproj",
    )(x2d, p["n1"], p["wg1"], p["wu1"], p["wd1"], p["n2"], p["w_in"],
      tables["rcos"], tables["rsin"], tables["acos"], tables["asin_lo"], tables["asin_hi"],
      p["qg"], p["kg"], tables["ones_bd"])


def _retention_kernel(q_ref, k_ref, v_ref, g_ref, beta_ref, inner_ref, qdec_ref, kdec_ref, cdec_ref,
                      o_ref):
    c = RET_KERNEL_CHUNK
    n_chunks = q_ref.shape[1] // c
    inner, qdec, kdec, cdec = inner_ref[0], qdec_ref[0], kdec_ref[0], cdec_ref[0]
    beta = beta_ref[...]
    rows = lambda i: slice(i * c, (i + 1) * c)

    increments = []
    for i in range(n_chunks - 1):
        kd = (k_ref[0, rows(i), :].astype(F32) * kdec).astype(BF16)
        increments.append(lax.dot_general(kd, v_ref[0, rows(i), :], TN, preferred_element_type=F32))
    scores = [lax.dot_general(q_ref[0, rows(i), :], k_ref[0, rows(i), :], NT,
                              preferred_element_type=F32) * inner
              for i in range(n_chunks)]
    states = [None]
    for i in range(n_chunks - 1):
        states.append(increments[i] if states[i] is None else states[i] * cdec + increments[i])
    for i in range(n_chunks):
        o = jnp.dot(scores[i].astype(BF16), v_ref[0, rows(i), :], preferred_element_type=F32)
        if states[i] is not None:
            o = o + jnp.dot(q_ref[0, rows(i), :], states[i].astype(BF16),
                            preferred_element_type=F32) * qdec
        g = g_ref[0, rows(i), :]
        o_ref[0, rows(i), :] = (_rms_rows(o) * (g * jax.nn.sigmoid(g)) * beta).astype(o_ref.dtype)


def _retention(rq, rk, rv, rg, beta, tables):
    b, s, _ = rq.shape
    d, c = RET_HEAD_DIM, RET_KERNEL_CHUNK
    head = lambda: pl.BlockSpec((1, s, d), lambda bi, hi: (bi, 0, hi))
    per_head = lambda r, w: pl.BlockSpec((1, r, w), lambda bi, hi: (hi, 0, 0))
    return pl.pallas_call(
        _retention_kernel,
        grid=(b, RET_HEADS),
        in_specs=[head(), head(), head(), head(),
                  pl.BlockSpec((1, d), lambda bi, hi: (0, hi)),
                  per_head(c, c), per_head(c, d), per_head(c, d), per_head(1, d)],
        out_specs=head(),
        out_shape=jax.ShapeDtypeStruct((b, s, RET_WIDTH), BF16),
        compiler_params=_params(),
        name="retention",
    )(rq, rk, rv, rg, beta, tables["inner_decay"], tables["q_decay"], tables["k_decay"],
      tables["chunk_decay"])


ONES_ROWS = 16
STAT_ROWS = 8
KMEAN_ROWS = 16


def _moba_kernel(q_ref, k_ref, v_ref, beta_ref, o_ref, vt_ref):
    bs = MOBA_BLOCK
    seq = k_ref.shape[1]
    nb = seq // bs
    dh = ATT_HEAD_DIM
    L = V7X_LANES
    heads = range(HEADS_PER_LANE_TILE)

    vt = jnp.transpose(v_ref[0].astype(F32))
    for h in heads:
        for n in range(nb):
            vt_ref[h, n, :dh, :] = vt[h * dh:(h + 1) * dh, n * bs:(n + 1) * bs].astype(BF16)
            vt_ref[h, n, dh:, :] = jnp.ones((ONES_ROWS, bs), BF16)
    kmean = jnp.mean(k_ref[0].astype(F32).reshape(nb, bs, L), axis=1)
    kmean = jnp.concatenate([kmean, jnp.zeros((KMEAN_ROWS - nb, L), F32)], axis=0)
    km_hi, km_lo = _split_bf16(kmean)

    lane = lax.broadcasted_iota(jnp.int32, (bs, L), 1)
    head_lanes = [(lane >= h * dh) & (lane < (h + 1) * dh) for h in heads]
    row_id = lax.broadcasted_iota(jnp.int32, (KMEAN_ROWS, bs), 0)
    causal_t = (lax.broadcasted_iota(jnp.int32, (bs, bs), 0)
                <= lax.broadcasted_iota(jnp.int32, (bs, bs), 1))
    stat = lambda t: jnp.broadcast_to(t, (STAT_ROWS, bs))
    wide = lambda t: jnp.broadcast_to(t[:1], (dh, bs))

    for j in range(nb):
        q = q_ref[0, j * bs:(j + 1) * bs, :]
        qh = [jnp.where(head_lanes[h], q, jnp.zeros_like(q)) for h in heads]
        chains = [(h, n) for n in range(j + 1) for h in heads]
        scores = [lax.dot_general(k_ref[0, n * bs:(n + 1) * bs, :], qh[h], NT,
                                  preferred_element_type=F32) for h, n in chains]
        probs = []
        for (h, n), st in zip(chains, scores):
            if n == j:
                st = jnp.where(causal_t, st, -jnp.inf)
            mt = jnp.max(st, axis=0, keepdims=True)
            probs.append((mt, jnp.exp2(st - mt).astype(BF16)))
        pv = {}
        for (h, n), (mt, pt) in zip(chains, probs):
            pv[h, n] = (stat(mt), jnp.dot(vt_ref[h, n], pt, preferred_element_type=F32))

        normed = []
        for h in heads:
            sel = [None] * j
            if j > MOBA_TOPK:
                gate = (lax.dot_general(km_hi, qh[h], NT, preferred_element_type=F32)
                        + lax.dot_general(km_lo, qh[h], NT, preferred_element_type=F32))
                gate = jnp.where(row_id < j, gate, -jnp.inf)
                rank = jnp.zeros(gate.shape, jnp.int32)
                for m in range(j):
                    other = gate[m:m + 1, :]
                    ahead = (other > gate) | ((other == gate) & (m < row_id))
                    rank = rank + ahead.astype(jnp.int32)
                keep = (rank < MOBA_TOPK).astype(F32)
                sel = [stat(keep[n:n + 1, :]) > 0.5 for n in range(j)]
            m_all = pv[h, j][0]
            for n in range(j):
                m_n = pv[h, n][0]
                m_all = jnp.maximum(m_all, m_n if sel[n] is None else jnp.where(sel[n], m_n, -jnp.inf))
            l_sum = jnp.zeros((STAT_ROWS, bs), F32)
            acc = jnp.zeros((dh, bs), F32)
            for n in range(j + 1):
                m_n, ov = pv[h, n]
                w = jnp.exp2(m_n - m_all)
                if n < j and sel[n] is not None:
                    w = jnp.where(sel[n], w, 0.0)
                l_sum = l_sum + w * ov[dh:dh + STAT_ROWS, :]
                acc = acc + wide(w) * ov[:dh, :]
            res = acc * wide(1.0 / l_sum)
            ms = jnp.mean(res * res, axis=0, keepdims=True)
            normed.append(res * lax.rsqrt(ms + NORM_EPS))
        att = jnp.transpose(jnp.concatenate(normed, axis=0))
        o_ref[0, j * bs:(j + 1) * bs, :] = (att * beta_ref[...]).astype(o_ref.dtype)


def _moba(aq, ak, av, beta):
    b, s, _ = aq.shape
    L = V7X_LANES
    nb = s // MOBA_BLOCK
    assert nb <= KMEAN_ROWS
    pair = lambda: pl.BlockSpec((1, s, L), lambda bi, hp: (bi, 0, hp))
    return pl.pallas_call(
        _moba_kernel,
        grid=(b, ATT_LANE_TILES),
        in_specs=[pair(), pair(), pair(), pl.BlockSpec((1, L), lambda bi, hp: (0, hp))],
        out_specs=pair(),
        out_shape=jax.ShapeDtypeStruct((b, s, ATT_WIDTH), BF16),
        scratch_shapes=[pltpu.VMEM((HEADS_PER_LANE_TILE, nb, ATT_HEAD_DIM + ONES_ROWS, MOBA_BLOCK),
                                   BF16)],
        compiler_params=_params(),
        name="moba",
    )(aq, ak, av, beta)


def _outproj_ffn_kernel(x1_ref, ret_ref, att_ref, wo_ret_ref, wo_att_ref, n_ref, wg_ref, wu_ref, wd_ref,
                        o_ref):
    x2 = (x1_ref[...]
          + jnp.dot(ret_ref[...], wo_ret_ref[...], preferred_element_type=F32)
          + jnp.dot(att_ref[...], wo_att_ref[...], preferred_element_type=F32))
    o_ref[...] = _swiglu_half_step(x2, n_ref[...], wg_ref, wu_ref, wd_ref)


def _outproj_ffn(x1, ret, att, seq, p):
    tokens = x1.shape[0]
    tm = _tiles(seq)["row_tile"]
    row = lambda w: pl.BlockSpec((tm, w), lambda i: (i, 0))
    return pl.pallas_call(
        _outproj_ffn_kernel,
        grid=(tokens // tm,),
        in_specs=[row(D_MODEL), row(RET_WIDTH), row(ATT_WIDTH),
                  _resident((RET_WIDTH, D_MODEL)), _resident((ATT_WIDTH, D_MODEL)),
                  _resident((1, D_MODEL)),
                  _resident((D_MODEL, FFN_DIM)), _resident((D_MODEL, FFN_DIM)),
                  _resident((FFN_DIM, D_MODEL))],
        out_specs=row(D_MODEL),
        out_shape=jax.ShapeDtypeStruct((tokens, D_MODEL), F32),
        compiler_params=_params(),
        name="outproj_ffn2",
    )(x1, ret, att, p["wo_ret"], p["wo_att"], p["n3"], p["wg2"], p["wu2"], p["wd2"])


def _tables(seq):
    pos = jnp.arange(seq, dtype=F32)[:, None]

    def angles(theta, rot_dim):
        half = rot_dim // 2
        inv_freq = theta ** (-jnp.arange(half, dtype=F32) * 2.0 / rot_dim)
        return pos * inv_freq[None, :]

    ang = angles(RET_ROPE_BASE, RET_HEAD_DIM)
    rcos = jnp.concatenate([jnp.cos(ang), jnp.cos(ang)], axis=1)
    rsin = jnp.concatenate([-jnp.sin(ang), jnp.sin(ang)], axis=1)

    ang = angles(ATT_ROPE_BASE, ATT_ROT_DIM)
    half = ATT_ROT_DIM // 2
    pad = jnp.zeros((seq, ATT_HEAD_DIM - ATT_ROT_DIM), F32)
    zero_half = jnp.zeros((seq, half), F32)
    head_cos = jnp.concatenate([jnp.cos(ang), jnp.cos(ang), 1.0 + pad], axis=1)
    head_sin_lo = jnp.concatenate([-jnp.sin(ang), zero_half, pad], axis=1)
    head_sin_hi = jnp.concatenate([zero_half, jnp.sin(ang), pad], axis=1)
    rep = lambda t: jnp.tile(t, (1, HEADS_PER_LANE_TILE))

    c = RET_KERNEL_CHUNK
    log_g = jnp.log(1.0 - 2.0 ** (-5.0 - jnp.arange(RET_HEADS, dtype=F32)))
    i = jnp.arange(c, dtype=F32)
    diff = i[:, None] - i[None, :]
    inner = jnp.where(diff[None] >= 0,
                      jnp.exp(jnp.maximum(diff, 0.0)[None] * log_g[:, None, None]), 0.0)
    lanes = lambda t: jnp.broadcast_to(t[:, :, None], (RET_HEADS, t.shape[1], RET_HEAD_DIM))
    q_decay = lanes(jnp.exp((i[None, :] + 1.0) * log_g[:, None]))
    k_decay = lanes(jnp.exp((c - 1.0 - i[None, :]) * log_g[:, None]))
    chunk_decay = lanes(jnp.exp(c * log_g)[:, None])

    group = jnp.arange(V7X_MXU_DIM) // ATT_HEAD_DIM
    ones_bd = (group[:, None] == group[None, :]).astype(BF16)
    return dict(rcos=rcos, rsin=rsin, acos=rep(head_cos), asin_lo=rep(head_sin_lo),
                asin_hi=rep(head_sin_hi), inner_decay=inner, q_decay=q_decay, k_decay=k_decay,
                chunk_decay=chunk_decay, ones_bd=ones_bd)


def kernel(x, ffn1_norm_w, ffn1_w_gate, ffn1_w_up, ffn1_w_down, mix_norm_w, w_in, ret_out_beta,
           q_norm_w, k_norm_w, att_out_beta, w_out, ffn2_norm_w, ffn2_w_gate, ffn2_w_up, ffn2_w_down):
    b, s, d = x.shape
    assert d == D_MODEL
    tables = _tables(s)
    depth = ffn1_norm_w.shape[0]
    q_logit_scale = ATT_HEAD_DIM ** -0.5 * math.log2(math.e)
    x2d = x.reshape(b * s, d)
    for l in range(depth):
        p = dict(
            n1=ffn1_norm_w[l][None, :], wg1=ffn1_w_gate[l].astype(BF16), wu1=ffn1_w_up[l].astype(BF16),
            wd1=ffn1_w_down[l].astype(BF16), n2=mix_norm_w[l][None, :], w_in=w_in[l].astype(BF16),
            qg=jnp.tile(q_norm_w[l] * q_logit_scale, ATT_HEADS)[None, :],
            kg=jnp.tile(k_norm_w[l], ATT_HEADS)[None, :],
            wo_ret=w_out[l][:RET_WIDTH].astype(BF16), wo_att=w_out[l][RET_WIDTH:].astype(BF16),
            n3=ffn2_norm_w[l][None, :], wg2=ffn2_w_gate[l].astype(BF16), wu2=ffn2_w_up[l].astype(BF16),
            wd2=ffn2_w_down[l].astype(BF16))
        x1, rq, rk, rv, rg, aq, ak, av = _ffn_inproj(x2d, s, p, tables)
        seq3 = lambda t: t.reshape(b, s, t.shape[-1])
        ret = _retention(seq3(rq), seq3(rk), seq3(rv), seq3(rg), ret_out_beta[l][None, :], tables)
        att = _moba(seq3(aq), seq3(ak), seq3(av), att_out_beta[l][None, :])
        x2d = _outproj_ffn(x1, ret.reshape(b * s, RET_WIDTH), att.reshape(b * s, ATT_WIDTH), s, p)
    return x2d.reshape(b, s, d)
```

```python
import math

import jax
import jax.numpy as jnp
from jax import lax
from jax.experimental import pallas as pl
from jax.experimental.pallas import tpu as pltpu

D_MODEL = 1024
RET_HEADS = 4
RET_HEAD_DIM = 128
RET_WIDTH = RET_HEADS * RET_HEAD_DIM
RET_ROPE_BASE = 10000.0
ATT_HEADS = 8
ATT_HEAD_DIM = 64
ATT_WIDTH = ATT_HEADS * ATT_HEAD_DIM
ATT_ROT_DIM = ATT_HEAD_DIM // 4
ATT_ROPE_BASE = 500000.0
MOBA_BLOCK = 256
MOBA_TOPK = 3
FFN_DIM = 2816
NORM_EPS = 1e-6

V7X_LANES = 128
V7X_MXU_DIM = 256
V7X_VMEM_LIMIT_BYTES = 56 * 1024 * 1024

F32 = jnp.float32
BF16 = jnp.bfloat16
HEADS_PER_LANE_TILE = V7X_LANES // ATT_HEAD_DIM
ATT_LANE_TILES = ATT_WIDTH // V7X_LANES
RET_KERNEL_CHUNK = V7X_MXU_DIM

NT = (((1,), (1,)), ((), ()))
TN = (((0,), (0,)), ((), ()))


def _tiles(seq):
    row_tile = 512 if seq % 512 == 0 else 256
    assert seq % row_tile == 0 and seq % MOBA_BLOCK == 0 and seq % RET_KERNEL_CHUNK == 0
    return dict(row_tile=row_tile)


def _params():
    return pltpu.CompilerParams(vmem_limit_bytes=V7X_VMEM_LIMIT_BYTES)


def _resident(shape):
    zeros = (0,) * len(shape)
    return pl.BlockSpec(shape, lambda *_: zeros, pipeline_mode=pl.Buffered(1))


def _rms_rows(x):
    return x * lax.rsqrt(jnp.mean(x * x, axis=-1, keepdims=True) + NORM_EPS)


def _swiglu_half_step(x, norm_w, wg_ref, wu_ref, wd_ref):
    h = (_rms_rows(x) * norm_w).astype(BF16)
    g = jnp.dot(h, wg_ref[...], preferred_element_type=F32)
    u = jnp.dot(h, wu_ref[...], preferred_element_type=F32)
    a = (g * jax.nn.sigmoid(g) * u).astype(BF16)
    return x + 0.5 * jnp.dot(a, wd_ref[...], preferred_element_type=F32)


def _split_bf16(x):
    hi = x.astype(BF16)
    lo = (x - hi.astype(F32)).astype(BF16)
    return hi, lo


def _ffn_inproj_kernel(x_ref, n1_ref, wg_ref, wu_ref, wd_ref, n2_ref, win_ref,
                       rcos_ref, rsin_ref, acos_ref, asin_lo_ref, asin_hi_ref,
                       qg_ref, kg_ref, ones_ref,
                       x1_ref, rq_ref, rk_ref, rv_ref, rg_ref, aq_ref, ak_ref, av_ref):
    x1 = _swiglu_half_step(x_ref[...], n1_ref[...], wg_ref, wu_ref, wd_ref)
    x1_ref[...] = x1
    hn = (_rms_rows(x1) * n2_ref[...]).astype(BF16)

    def proj(col0, width):
        return jnp.dot(hn, win_ref[:, col0:col0 + width], preferred_element_type=F32)

    R, A, L = RET_WIDTH, ATT_WIDTH, V7X_LANES
    rcos, rsin = rcos_ref[...], rsin_ref[...]

    def ret_rope(p, out_ref, scale):
        for h in range(RET_HEADS):
            t = p[:, h * L:(h + 1) * L]
            r = t * rcos + pltpu.roll(t, RET_HEAD_DIM // 2, 1) * rsin
            if scale is not None:
                r = r * scale
            out_ref[:, h * L:(h + 1) * L] = r.astype(out_ref.dtype)

    ret_rope(proj(0, R), rq_ref, None)
    ret_rope(proj(R, R), rk_ref, RET_HEAD_DIM ** -0.5)
    rv_ref[...] = proj(2 * R, R).astype(rv_ref.dtype)
    rg_ref[...] = proj(3 * R, R).astype(rg_ref.dtype)

    acos, asin_lo, asin_hi = acos_ref[...], asin_lo_ref[...], asin_hi_ref[...]
    half_rot = ATT_ROT_DIM // 2

    def att_norm_rope(p, gain_ref, out_ref):
        for c in range(A // V7X_MXU_DIM):
            t = p[:, c * V7X_MXU_DIM:(c + 1) * V7X_MXU_DIM]
            hi, lo = _split_bf16(t * t)
            ss = (jnp.dot(hi, ones_ref[...], preferred_element_type=F32)
                  + jnp.dot(lo, ones_ref[...], preferred_element_type=F32))
            tn = t * lax.rsqrt(ss * (1.0 / ATT_HEAD_DIM) + NORM_EPS)
            tn = tn * gain_ref[:, c * V7X_MXU_DIM:(c + 1) * V7X_MXU_DIM]
            for s in range(V7X_MXU_DIM // L):
                u = tn[:, s * L:(s + 1) * L]
                r = (u * acos + pltpu.roll(u, half_rot, 1) * asin_hi
                     + pltpu.roll(u, L - half_rot, 1) * asin_lo)
                col = c * V7X_MXU_DIM + s * L
                out_ref[:, col:col + L] = r.astype(out_ref.dtype)

    o0 = 4 * R
    att_norm_rope(proj(o0, A), qg_ref, aq_ref)
    att_norm_rope(proj(o0 + A, A), kg_ref, ak_ref)
    av_ref[...] = proj(o0 + 2 * A, A).astype(av_ref.dtype)


def _ffn_inproj(x2d, seq, p, tables):
    tokens = x2d.shape[0]
    tm = _tiles(seq)["row_tile"]
    pos_blocks = seq // tm
    row = lambda w: pl.BlockSpec((tm, w), lambda i: (i, 0))
    pos = lambda: pl.BlockSpec((tm, V7X_LANES), lambda i: (i % pos_blocks, 0))
    in_width = p["w_in"].shape[1]
    out_shapes = ([jax.ShapeDtypeStruct((tokens, D_MODEL), F32)]
                  + [jax.ShapeDtypeStruct((tokens, RET_WIDTH), BF16)] * 3
                  + [jax.ShapeDtypeStruct((tokens, RET_WIDTH), F32)]
                  + [jax.ShapeDtypeStruct((tokens, ATT_WIDTH), BF16)] * 3)
    return pl.pallas_call(
        _ffn_inproj_kernel,
        grid=(tokens // tm,),
        in_specs=[row(D_MODEL), _resident((1, D_MODEL)),
                  _resident((D_MODEL, FFN_DIM)), _resident((D_MODEL, FFN_DIM)),
                  _resident((FFN_DIM, D_MODEL)), _resident((1, D_MODEL)),
                  _resident((D_MODEL, in_width)),
                  pos(), pos(), pos(), pos(), pos(),
                  _resident((1, ATT_WIDTH)), _resident((1, ATT_WIDTH)),
                  _resident((V7X_MXU_DIM, V7X_MXU_DIM))],
        out_specs=[row(D_MODEL)] + [row(RET_WIDTH)] * 4 + [row(ATT_WIDTH)] * 3,
        out_shape=out_shapes,
        compiler_params=_params(),
        name="ffn1_inproj",
    )(x2d, p["n1"], p["wg1"], p["wu1"], p["wd1"], p["n2"], p["w_in"],
      tables["rcos"], tables["rsin"], tables["acos"], tables["asin_lo"], tables["asin_hi"],
      p["qg"], p["kg"], tables["ones_bd"])


def _retention_stream(q_ref, k_ref, v_ref, g_ref, beta_ref, inner_ref, qdec_ref, kdec_ref, cdec_ref,
                      o_ref):
    c = RET_KERNEL_CHUNK
    n_chunks = q_ref.shape[1] // c
    rows = lambda i: slice(i * c, (i + 1) * c)
    live = {"state": None}

    def begin(i):
        live["scores"] = lax.dot_general(q_ref[0, rows(i), :], k_ref[0, rows(i), :], NT,
                                         preferred_element_type=F32) * inner_ref[0]
        if i < n_chunks - 1:
            kd = (k_ref[0, rows(i), :].astype(F32) * kdec_ref[0]).astype(BF16)
            live["increment"] = lax.dot_general(kd, v_ref[0, rows(i), :], TN,
                                                preferred_element_type=F32)

    def finish(i):
        o = jnp.dot(live.pop("scores").astype(BF16), v_ref[0, rows(i), :], preferred_element_type=F32)
        state = live["state"]
        if state is not None:
            o = o + jnp.dot(q_ref[0, rows(i), :], state.astype(BF16),
                            preferred_element_type=F32) * qdec_ref[0]
        if i < n_chunks - 1:
            inc = live.pop("increment")
            live["state"] = inc if state is None else state * cdec_ref[0] + inc
        g = g_ref[0, rows(i), :]
        o_ref[0, rows(i), :] = (_rms_rows(o) * (g * jax.nn.sigmoid(g)) * beta_ref[...]).astype(o_ref.dtype)

    return begin, finish


ONES_ROWS = 16
STAT_ROWS = 8
KMEAN_ROWS = 16
SCORE_LOOKAHEAD = 6


def _mixer_kernel(rq_ref, rk_ref, rv_ref, rg_ref, rbeta_ref, inner_ref, qdec_ref, kdec_ref, cdec_ref,
                  q_ref, k_ref, v_ref, beta_ref, ret_ref, o_ref, vt_ref):
    ret_begin, ret_finish = _retention_stream(rq_ref, rk_ref, rv_ref, rg_ref, rbeta_ref, inner_ref,
                                              qdec_ref, kdec_ref, cdec_ref, ret_ref)
    bs = MOBA_BLOCK
    seq = k_ref.shape[1]
    nb = seq // bs
    dh = ATT_HEAD_DIM
    L = V7X_LANES
    heads = range(HEADS_PER_LANE_TILE)

    vt = jnp.transpose(v_ref[0].astype(F32))
    for h in heads:
        for n in range(nb):
            vt_ref[h, n, :dh, :] = vt[h * dh:(h + 1) * dh, n * bs:(n + 1) * bs].astype(BF16)
            vt_ref[h, n, dh:, :] = jnp.ones((ONES_ROWS, bs), BF16)
    kmean = jnp.mean(k_ref[0].astype(F32).reshape(nb, bs, L), axis=1)
    kmean = jnp.concatenate([kmean, jnp.zeros((KMEAN_ROWS - nb, L), F32)], axis=0)
    km_hi, km_lo = _split_bf16(kmean)

    lane = lax.broadcasted_iota(jnp.int32, (bs, L), 1)
    head_lanes = [(lane >= h * dh) & (lane < (h + 1) * dh) for h in heads]
    row_id = lax.broadcasted_iota(jnp.int32, (KMEAN_ROWS, bs), 0)
    causal_t = (lax.broadcasted_iota(jnp.int32, (bs, bs), 0)
                <= lax.broadcasted_iota(jnp.int32, (bs, bs), 1))
    stat = lambda t: jnp.broadcast_to(t, (STAT_ROWS, bs))
    wide = lambda t: jnp.broadcast_to(t[:1], (dh, bs))

    masked_q = {}

    def query(j, h):
        if (j, h) not in masked_q:
            q = q_ref[0, j * bs:(j + 1) * bs, :]
            masked_q[j, h] = jnp.where(head_lanes[h], q, jnp.zeros_like(q))
        return masked_q[j, h]

    def combine(j, h, pv):
        sel = [None] * j
        if j > MOBA_TOPK:
            gate = (lax.dot_general(km_hi, query(j, h), NT, preferred_element_type=F32)
                    + lax.dot_general(km_lo, query(j, h), NT, preferred_element_type=F32))
            gate = jnp.where(row_id < j, gate, -jnp.inf)
            rank = jnp.zeros(gate.shape, jnp.int32)
            for m in range(j):
                other = gate[m:m + 1, :]
                ahead = (other > gate) | ((other == gate) & (m < row_id))
                rank = rank + ahead.astype(jnp.int32)
            keep = (rank < MOBA_TOPK).astype(F32)
            sel = [stat(keep[n:n + 1, :]) > 0.5 for n in range(j)]
        m_all = pv[j][0]
        for n in range(j):
            m_n = pv[n][0]
            m_all = jnp.maximum(m_all, m_n if sel[n] is None else jnp.where(sel[n], m_n, -jnp.inf))
        l_sum = jnp.zeros((STAT_ROWS, bs), F32)
        acc = jnp.zeros((dh, bs), F32)
        for n in range(j + 1):
            m_n, ov = pv[n]
            w = jnp.exp2(m_n - m_all)
            if n < j and sel[n] is not None:
                w = jnp.where(sel[n], w, 0.0)
            l_sum = l_sum + w * ov[dh:dh + STAT_ROWS, :]
            acc = acc + wide(w) * ov[:dh, :]
        res = acc * wide(1.0 / l_sum)
        ms = jnp.mean(res * res, axis=0, keepdims=True)
        return res * lax.rsqrt(ms + NORM_EPS)

    work = [(j, n, h) for j in range(nb) for n in range(j + 1) for h in heads]
    scores, pv, normed = {}, {}, {}
    for step in range(-SCORE_LOOKAHEAD, len(work)):
        if step + SCORE_LOOKAHEAD < len(work):
            j, n, h = work[step + SCORE_LOOKAHEAD]
            scores[j, n, h] = lax.dot_general(k_ref[0, n * bs:(n + 1) * bs, :], query(j, h), NT,
                                              preferred_element_type=F32)
        if step < 0:
            continue
        j, n, h = work[step]
        if n == 0 and h == heads[0]:
            ret_begin(j)
        st = scores.pop((j, n, h))
        if n == j:
            st = jnp.where(causal_t, st, -jnp.inf)
        mt = jnp.max(st, axis=0, keepdims=True)
        pt = jnp.exp2(st - mt).astype(BF16)
        pv.setdefault((j, h), {})[n] = (stat(mt), jnp.dot(vt_ref[h, n], pt, preferred_element_type=F32))
        if n == j:
            normed[j, h] = combine(j, h, pv.pop((j, h)))
            if h == heads[-1]:
                att = jnp.transpose(jnp.concatenate([normed.pop((j, g)) for g in heads], axis=0))
                o_ref[0, j * bs:(j + 1) * bs, :] = (att * beta_ref[...]).astype(o_ref.dtype)
                ret_finish(j)


def _mixers(rq, rk, rv, rg, ret_beta, aq, ak, av, att_beta, tables):
    b, s, _ = aq.shape
    L = V7X_LANES
    nb = s // MOBA_BLOCK
    c = RET_KERNEL_CHUNK
    assert nb <= KMEAN_ROWS
    assert RET_HEADS == ATT_LANE_TILES and RET_HEAD_DIM == L and c == MOBA_BLOCK
    cols = lambda: pl.BlockSpec((1, s, L), lambda bi, g: (bi, 0, g))
    gain = lambda: pl.BlockSpec((1, L), lambda bi, g: (0, g))
    per_head = lambda r, w: pl.BlockSpec((1, r, w), lambda bi, g: (g, 0, 0))
    return pl.pallas_call(
        _mixer_kernel,
        grid=(b, ATT_LANE_TILES),
        in_specs=[cols(), cols(), cols(), cols(), gain(),
                  per_head(c, c), per_head(c, L), per_head(c, L), per_head(1, L),
                  cols(), cols(), cols(), gain()],
        out_specs=[cols(), cols()],
        out_shape=[jax.ShapeDtypeStruct((b, s, RET_WIDTH), BF16),
                   jax.ShapeDtypeStruct((b, s, ATT_WIDTH), BF16)],
        scratch_shapes=[pltpu.VMEM((HEADS_PER_LANE_TILE, nb, ATT_HEAD_DIM + ONES_ROWS, MOBA_BLOCK),
                                   BF16)],
        compiler_params=_params(),
        name="mixers",
    )(rq, rk, rv, rg, ret_beta, tables["inner_decay"], tables["q_decay"], tables["k_decay"],
      tables["chunk_decay"], aq, ak, av, att_beta)


def _outproj_ffn_kernel(x1_ref, ret_ref, att_ref, wo_ret_ref, wo_att_ref, n_ref, wg_ref, wu_ref, wd_ref,
                        o_ref):
    x2 = (x1_ref[...]
          + jnp.dot(ret_ref[...], wo_ret_ref[...], preferred_element_type=F32)
          + jnp.dot(att_ref[...], wo_att_ref[...], preferred_element_type=F32))
    o_ref[...] = _swiglu_half_step(x2, n_ref[...], wg_ref, wu_ref, wd_ref)


def _outproj_ffn(x1, ret, att, seq, p):
    tokens = x1.shape[0]
    tm = _tiles(seq)["row_tile"]
    row = lambda w: pl.BlockSpec((tm, w), lambda i: (i, 0))
    return pl.pallas_call(
        _outproj_ffn_kernel,
        grid=(tokens // tm,),
        in_specs=[row(D_MODEL), row(RET_WIDTH), row(ATT_WIDTH),
                  _resident((RET_WIDTH, D_MODEL)), _resident((ATT_WIDTH, D_MODEL)),
                  _resident((1, D_MODEL)),
                  _resident((D_MODEL, FFN_DIM)), _resident((D_MODEL, FFN_DIM)),
                  _resident((FFN_DIM, D_MODEL))],
        out_specs=row(D_MODEL),
        out_shape=jax.ShapeDtypeStruct((tokens, D_MODEL), F32),
        compiler_params=_params(),
        name="outproj_ffn2",
    )(x1, ret, att, p["wo_ret"], p["wo_att"], p["n3"], p["wg2"], p["wu2"], p["wd2"])


def _tables(seq):
    pos = jnp.arange(seq, dtype=F32)[:, None]

    def angles(theta, rot_dim):
        half = rot_dim // 2
        inv_freq = theta ** (-jnp.arange(half, dtype=F32) * 2.0 / rot_dim)
        return pos * inv_freq[None, :]

    ang = angles(RET_ROPE_BASE, RET_HEAD_DIM)
    rcos = jnp.concatenate([jnp.cos(ang), jnp.cos(ang)], axis=1)
    rsin = jnp.concatenate([-jnp.sin(ang), jnp.sin(ang)], axis=1)

    ang = angles(ATT_ROPE_BASE, ATT_ROT_DIM)
    half = ATT_ROT_DIM // 2
    pad = jnp.zeros((seq, ATT_HEAD_DIM - ATT_ROT_DIM), F32)
    zero_half = jnp.zeros((seq, half), F32)
    head_cos = jnp.concatenate([jnp.cos(ang), jnp.cos(ang), 1.0 + pad], axis=1)
    head_sin_lo = jnp.concatenate([-jnp.sin(ang), zero_half, pad], axis=1)
    head_sin_hi = jnp.concatenate([zero_half, jnp.sin(ang), pad], axis=1)
    rep = lambda t: jnp.tile(t, (1, HEADS_PER_LANE_TILE))

    c = RET_KERNEL_CHUNK
    log_g = jnp.log(1.0 - 2.0 ** (-5.0 - jnp.arange(RET_HEADS, dtype=F32)))
    i = jnp.arange(c, dtype=F32)
    diff = i[:, None] - i[None, :]
    inner = jnp.where(diff[None] >= 0,
                      jnp.exp(jnp.maximum(diff, 0.0)[None] * log_g[:, None, None]), 0.0)
    lanes = lambda t: jnp.broadcast_to(t[:, :, None], (RET_HEADS, t.shape[1], RET_HEAD_DIM))
    q_decay = lanes(jnp.exp((i[None, :] + 1.0) * log_g[:, None]))
    k_decay = lanes(jnp.exp((c - 1.0 - i[None, :]) * log_g[:, None]))
    chunk_decay = lanes(jnp.exp(c * log_g)[:, None])

    group = jnp.arange(V7X_MXU_DIM) // ATT_HEAD_DIM
    ones_bd = (group[:, None] == group[None, :]).astype(BF16)
    return dict(rcos=rcos, rsin=rsin, acos=rep(head_cos), asin_lo=rep(head_sin_lo),
                asin_hi=rep(head_sin_hi), inner_decay=inner, q_decay=q_decay, k_decay=k_decay,
                chunk_decay=chunk_decay, ones_bd=ones_bd)


def kernel(x, ffn1_norm_w, ffn1_w_gate, ffn1_w_up, ffn1_w_down, mix_norm_w, w_in, ret_out_beta,
           q_norm_w, k_norm_w, att_out_beta, w_out, ffn2_norm_w, ffn2_w_gate, ffn2_w_up, ffn2_w_down):
    b, s, d = x.shape
    assert d == D_MODEL
    tables = _tables(s)
    depth = ffn1_norm_w.shape[0]
    q_logit_scale = ATT_HEAD_DIM ** -0.5 * math.log2(math.e)
    x2d = x.reshape(b * s, d)
    for l in range(depth):
        p = dict(
            n1=ffn1_norm_w[l][None, :], wg1=ffn1_w_gate[l].astype(BF16), wu1=ffn1_w_up[l].astype(BF16),
            wd1=ffn1_w_down[l].astype(BF16), n2=mix_norm_w[l][None, :], w_in=w_in[l].astype(BF16),
            qg=jnp.tile(q_norm_w[l] * q_logit_scale, ATT_HEADS)[None, :],
            kg=jnp.tile(k_norm_w[l], ATT_HEADS)[None, :],
            wo_ret=w_out[l][:RET_WIDTH].astype(BF16), wo_att=w_out[l][RET_WIDTH:].astype(BF16),
            n3=ffn2_norm_w[l][None, :], wg2=ffn2_w_gate[l].astype(BF16), wu2=ffn2_w_up[l].astype(BF16),
            wd2=ffn2_w_down[l].astype(BF16))
        x1, rq, rk, rv, rg, aq, ak, av = _ffn_inproj(x2d, s, p, tables)
        seq3 = lambda t: t.reshape(b, s, t.shape[-1])
        ret, att = _mixers(seq3(rq), seq3(rk), seq3(rv), seq3(rg), ret_out_beta[l][None, :],
                           seq3(aq), seq3(ak), seq3(av), att_out_beta[l][None, :], tables)
        x2d = _outproj_ffn(x1, ret.reshape(b * s, RET_WIDTH), att.reshape(b * s, ATT_WIDTH), s, p)
    return x2d.reshape(b, s, d)
```

```python
import math

import jax
import jax.numpy as jnp
from jax import lax
from jax.experimental import pallas as pl
from jax.experimental.pallas import tpu as pltpu

D_MODEL = 1024
RET_HEADS = 4
RET_HEAD_DIM = 128
RET_WIDTH = RET_HEADS * RET_HEAD_DIM
RET_ROPE_BASE = 10000.0
ATT_HEADS = 8
ATT_HEAD_DIM = 64
ATT_WIDTH = ATT_HEADS * ATT_HEAD_DIM
ATT_ROT_DIM = ATT_HEAD_DIM // 4
ATT_ROPE_BASE = 500000.0
MOBA_BLOCK = 256
MOBA_TOPK = 3
FFN_DIM = 2816
NORM_EPS = 1e-6

V7X_LANES = 128
V7X_MXU_DIM = 256
V7X_VMEM_LIMIT_BYTES = 56 * 1024 * 1024

F32 = jnp.float32
BF16 = jnp.bfloat16
HEADS_PER_LANE_TILE = V7X_LANES // ATT_HEAD_DIM
ATT_LANE_TILES = ATT_WIDTH // V7X_LANES
RET_KERNEL_CHUNK = V7X_MXU_DIM

NT = (((1,), (1,)), ((), ()))
TN = (((0,), (0,)), ((), ()))


def _tiles(seq):
    row_tile = 512 if seq % 512 == 0 else 256
    assert seq % row_tile == 0 and seq % MOBA_BLOCK == 0 and seq % RET_KERNEL_CHUNK == 0
    return dict(row_tile=row_tile)


def _params(dimension_semantics=None):
    return pltpu.CompilerParams(vmem_limit_bytes=V7X_VMEM_LIMIT_BYTES,
                                dimension_semantics=dimension_semantics)


def _resident(shape):
    zeros = (0,) * len(shape)
    return pl.BlockSpec(shape, lambda *_: zeros, pipeline_mode=pl.Buffered(1))


def _rms_rows(x):
    return x * lax.rsqrt(jnp.mean(x * x, axis=-1, keepdims=True) + NORM_EPS)


def _swiglu_activation(h, wg_ref, wu_ref):
    g = jnp.dot(h, wg_ref[...], preferred_element_type=F32)
    u = jnp.dot(h, wu_ref[...], preferred_element_type=F32)
    return (g * jax.nn.sigmoid(g) * u).astype(BF16)


def _swiglu_half_step(x, norm_w, wg_ref, wu_ref, wd_ref):
    h = (_rms_rows(x) * norm_w).astype(BF16)
    a = _swiglu_activation(h, wg_ref, wu_ref)
    return x + 0.5 * jnp.dot(a, wd_ref[...], preferred_element_type=F32)


def _split_bf16(x):
    hi = x.astype(BF16)
    lo = (x - hi.astype(F32)).astype(BF16)
    return hi, lo


def _ffn_inproj_kernel(x_ref, n1_ref, wg_ref, wu_ref, wd_ref, n2_ref, win_ref,
                       rcos_ref, rsin_ref, acos_ref, asin_lo_ref, asin_hi_ref,
                       qg_ref, kg_ref,
                       x1_ref, rq_ref, rk_ref, rv_ref, rg_ref, aq_ref, ak_ref, av_ref):
    x1 = _swiglu_half_step(x_ref[...], n1_ref[...], wg_ref, wu_ref, wd_ref)
    x1_ref[...] = x1
    hn = (_rms_rows(x1) * n2_ref[...]).astype(BF16)

    def proj(col0, width):
        return jnp.dot(hn, win_ref[:, col0:col0 + width], preferred_element_type=F32)

    R, A, L = RET_WIDTH, ATT_WIDTH, V7X_LANES
    rcos, rsin = rcos_ref[...], rsin_ref[...]

    def ret_rope(p, out_ref, scale):
        for h in range(RET_HEADS):
            t = p[:, h * L:(h + 1) * L]
            r = t * rcos + pltpu.roll(t, RET_HEAD_DIM // 2, 1) * rsin
            if scale is not None:
                r = r * scale
            out_ref[:, h * L:(h + 1) * L] = r.astype(out_ref.dtype)

    acos, asin_lo, asin_hi = acos_ref[...], asin_lo_ref[...], asin_hi_ref[...]
    half_rot = ATT_ROT_DIM // 2

    first_head = lax.broadcasted_iota(jnp.int32, (x1.shape[0], L), 1) < ATT_HEAD_DIM

    def att_norm_rope(p, gain_ref, out_ref):
        for c in range(A // L):
            t = p[:, c * L:(c + 1) * L]
            sq = t * t
            ss = jnp.where(first_head,
                           jnp.sum(jnp.where(first_head, sq, 0.0), axis=-1, keepdims=True),
                           jnp.sum(jnp.where(first_head, 0.0, sq), axis=-1, keepdims=True))
            u = t * lax.rsqrt(ss * (1.0 / ATT_HEAD_DIM) + NORM_EPS) * gain_ref[:, c * L:(c + 1) * L]
            r = (u * acos + pltpu.roll(u, half_rot, 1) * asin_hi
                 + pltpu.roll(u, L - half_rot, 1) * asin_lo)
            out_ref[:, c * L:(c + 1) * L] = r.astype(out_ref.dtype)

    o0 = 4 * R
    att_norm_rope(proj(o0, A), qg_ref, aq_ref)
    att_norm_rope(proj(o0 + A, A), kg_ref, ak_ref)
    ret_rope(proj(0, R), rq_ref, None)
    ret_rope(proj(R, R), rk_ref, RET_HEAD_DIM ** -0.5)
    rv_ref[...] = proj(2 * R, R).astype(rv_ref.dtype)
    rg_ref[...] = proj(3 * R, R).astype(rg_ref.dtype)
    av_ref[...] = proj(o0 + 2 * A, A).astype(av_ref.dtype)


def _ffn_inproj(x2d, seq, p, tables):
    tokens = x2d.shape[0]
    tm = _tiles(seq)["row_tile"]
    pos_blocks = seq // tm
    row = lambda w: pl.BlockSpec((tm, w), lambda i: (i, 0))
    pos = lambda: pl.BlockSpec((tm, V7X_LANES), lambda i: (i % pos_blocks, 0))
    in_width = p["w_in"].shape[1]
    out_shapes = ([jax.ShapeDtypeStruct((tokens, D_MODEL), F32)]
                  + [jax.ShapeDtypeStruct((tokens, RET_WIDTH), BF16)] * 3
                  + [jax.ShapeDtypeStruct((tokens, RET_WIDTH), F32)]
                  + [jax.ShapeDtypeStruct((tokens, ATT_WIDTH), BF16)] * 3)
    return pl.pallas_call(
        _ffn_inproj_kernel,
        grid=(tokens // tm,),
        in_specs=[row(D_MODEL), _resident((1, D_MODEL)),
                  _resident((D_MODEL, FFN_DIM)), _resident((D_MODEL, FFN_DIM)),
                  _resident((FFN_DIM, D_MODEL)), _resident((1, D_MODEL)),
                  _resident((D_MODEL, in_width)),
                  pos(), pos(), pos(), pos(), pos(),
                  _resident((1, ATT_WIDTH)), _resident((1, ATT_WIDTH))],
        out_specs=[row(D_MODEL)] + [row(RET_WIDTH)] * 4 + [row(ATT_WIDTH)] * 3,
        out_shape=out_shapes,
        compiler_params=_params(),
        name="ffn1_inproj",
    )(x2d, p["n1"], p["wg1"], p["wu1"], p["wd1"], p["n2"], p["w_in"],
      tables["rcos"], tables["rsin"], tables["acos"], tables["asin_lo"], tables["asin_hi"],
      p["qg"], p["kg"])


def _retention_stream(q_ref, k_ref, v_ref, g_ref, beta_ref, inner_ref, qdec_ref, kdec_ref, cdec_ref,
                      o_ref):
    c = RET_KERNEL_CHUNK
    n_chunks = q_ref.shape[1] // c
    rows = lambda i: slice(i * c, (i + 1) * c)
    live = {"state": None}

    def begin(i):
        live["scores", i] = lax.dot_general(q_ref[0, rows(i), :], k_ref[0, rows(i), :], NT,
                                            preferred_element_type=F32) * inner_ref[0]
        if i < n_chunks - 1:
            kd = (k_ref[0, rows(i), :].astype(F32) * kdec_ref[0]).astype(BF16)
            live["increment", i] = lax.dot_general(kd, v_ref[0, rows(i), :], TN,
                                                   preferred_element_type=F32)

    def finish(i):
        o = jnp.dot(live.pop(("scores", i)).astype(BF16), v_ref[0, rows(i), :],
                    preferred_element_type=F32)
        state = live["state"]
        if state is not None:
            o = o + jnp.dot(q_ref[0, rows(i), :], state.astype(BF16),
                            preferred_element_type=F32) * qdec_ref[0]
        if i < n_chunks - 1:
            inc = live.pop(("increment", i))
            live["state"] = inc if state is None else state * cdec_ref[0] + inc
        g = g_ref[0, rows(i), :]
        o_ref[0, rows(i), :] = (_rms_rows(o) * (g * jax.nn.sigmoid(g)) * beta_ref[...]).astype(o_ref.dtype)

    return begin, finish


ONES_ROWS = 16
STAT_ROWS = 8
KMEAN_ROWS = 16
SCORE_LOOKAHEAD = 6
PV_DELAY = 0


def _mixer_kernel(rq_ref, rk_ref, rv_ref, rg_ref, rbeta_ref, inner_ref, qdec_ref, kdec_ref, cdec_ref,
                  q_ref, k_ref, v_ref, beta_ref, ret_ref, o_ref, vt_ref):
    ret_begin, ret_finish = _retention_stream(rq_ref, rk_ref, rv_ref, rg_ref, rbeta_ref, inner_ref,
                                              qdec_ref, kdec_ref, cdec_ref, ret_ref)
    bs = MOBA_BLOCK
    seq = k_ref.shape[1]
    nb = seq // bs
    dh = ATT_HEAD_DIM
    L = V7X_LANES
    heads = range(HEADS_PER_LANE_TILE)

    vt = jnp.transpose(v_ref[0].astype(F32))
    for h in heads:
        for n in range(nb):
            vt_ref[h, n, :dh, :] = vt[h * dh:(h + 1) * dh, n * bs:(n + 1) * bs].astype(BF16)
            vt_ref[h, n, dh:, :] = jnp.ones((ONES_ROWS, bs), BF16)
    kmean = jnp.mean(k_ref[0].astype(F32).reshape(nb, bs, L), axis=1)
    kmean = jnp.concatenate([kmean, jnp.zeros((KMEAN_ROWS - nb, L), F32)], axis=0)
    km_hi, km_lo = _split_bf16(kmean)

    lane = lax.broadcasted_iota(jnp.int32, (bs, L), 1)
    head_lanes = [(lane >= h * dh) & (lane < (h + 1) * dh) for h in heads]
    row_id = lax.broadcasted_iota(jnp.int32, (KMEAN_ROWS, bs), 0)
    causal_t = (lax.broadcasted_iota(jnp.int32, (bs, bs), 0)
                <= lax.broadcasted_iota(jnp.int32, (bs, bs), 1))
    stat = lambda t: jnp.broadcast_to(t, (STAT_ROWS, bs))
    wide = lambda t: jnp.broadcast_to(t[:1], (dh, bs))

    masked_q = {}

    def query(j, h):
        if (j, h) not in masked_q:
            q = q_ref[0, j * bs:(j + 1) * bs, :]
            masked_q[j, h] = jnp.where(head_lanes[h], q, jnp.zeros_like(q))
        return masked_q[j, h]

    def combine(j, h, pv):
        sel = [None] * j
        if j > MOBA_TOPK:
            gate = (lax.dot_general(km_hi, query(j, h), NT, preferred_element_type=F32)
                    + lax.dot_general(km_lo, query(j, h), NT, preferred_element_type=F32))
            gate = jnp.where(row_id < j, gate, -jnp.inf)
            rank = jnp.zeros(gate.shape, jnp.int32)
            for m in range(j):
                other = gate[m:m + 1, :]
                ahead = (other > gate) | ((other == gate) & (m < row_id))
                rank = rank + ahead.astype(jnp.int32)
            keep = (rank < MOBA_TOPK).astype(F32)
            sel = [stat(keep[n:n + 1, :]) > 0.5 for n in range(j)]
        m_all = pv[j][0]
        for n in range(j):
            m_n = pv[n][0]
            m_all = jnp.maximum(m_all, m_n if sel[n] is None else jnp.where(sel[n], m_n, -jnp.inf))
        l_sum = jnp.zeros((STAT_ROWS, bs), F32)
        acc = jnp.zeros((dh, bs), F32)
        for n in range(j + 1):
            m_n, ov = pv[n]
            w = jnp.exp2(m_n - m_all)
            if n < j and sel[n] is not None:
                w = jnp.where(sel[n], w, 0.0)
            l_sum = l_sum + w * ov[dh:dh + STAT_ROWS, :]
            acc = acc + wide(w) * ov[:dh, :]
        res = acc * wide(1.0 / l_sum)
        ms = jnp.mean(res * res, axis=0, keepdims=True)
        return res * lax.rsqrt(ms + NORM_EPS)

    work = [(j, n, h) for j in range(nb) for n in range(j + 1) for h in heads]
    scores, probs, pv, normed = {}, {}, {}, {}
    for step in range(-SCORE_LOOKAHEAD, len(work) + PV_DELAY):
        if step + SCORE_LOOKAHEAD < len(work):
            j, n, h = work[step + SCORE_LOOKAHEAD]
            scores[j, n, h] = lax.dot_general(k_ref[0, n * bs:(n + 1) * bs, :], query(j, h), NT,
                                              preferred_element_type=F32)
        if 0 <= step < len(work):
            j, n, h = work[step]
            if n == 0 and h == heads[0]:
                ret_begin(j)
            st = scores.pop((j, n, h))
            if n == j:
                st = jnp.where(causal_t, st, -jnp.inf)
            mt = jnp.max(st, axis=0, keepdims=True)
            probs[j, n, h] = (stat(mt), jnp.exp2(st - mt).astype(BF16))
        if 0 <= step - PV_DELAY < len(work):
            j, n, h = work[step - PV_DELAY]
            mt, pt = probs.pop((j, n, h))
            pv.setdefault((j, h), {})[n] = (mt, jnp.dot(vt_ref[h, n], pt, preferred_element_type=F32))
            if n == j:
                normed[j, h] = combine(j, h, pv.pop((j, h)))
                if h == heads[-1]:
                    att = jnp.transpose(jnp.concatenate([normed.pop((j, g)) for g in heads], axis=0))
                    o_ref[0, j * bs:(j + 1) * bs, :] = (att * beta_ref[...]).astype(o_ref.dtype)
                    ret_finish(j)


def _mixers(rq, rk, rv, rg, ret_beta, aq, ak, av, att_beta, tables):
    b, s, _ = aq.shape
    L = V7X_LANES
    nb = s // MOBA_BLOCK
    c = RET_KERNEL_CHUNK
    assert nb <= KMEAN_ROWS
    assert RET_HEADS == ATT_LANE_TILES and RET_HEAD_DIM == L and c == MOBA_BLOCK
    cols = lambda: pl.BlockSpec((1, s, L), lambda bi, g: (bi, 0, g))
    gain = lambda: pl.BlockSpec((1, L), lambda bi, g: (0, g))
    per_head = lambda r, w: pl.BlockSpec((1, r, w), lambda bi, g: (g, 0, 0))
    return pl.pallas_call(
        _mixer_kernel,
        grid=(b, ATT_LANE_TILES),
        in_specs=[cols(), cols(), cols(), cols(), gain(),
                  per_head(c, c), per_head(c, L), per_head(c, L), per_head(1, L),
                  cols(), cols(), cols(), gain()],
        out_specs=[cols(), cols()],
        out_shape=[jax.ShapeDtypeStruct((b, s, RET_WIDTH), BF16),
                   jax.ShapeDtypeStruct((b, s, ATT_WIDTH), BF16)],
        scratch_shapes=[pltpu.VMEM((HEADS_PER_LANE_TILE, nb, ATT_HEAD_DIM + ONES_ROWS, MOBA_BLOCK),
                                   BF16)],
        compiler_params=_params(),
        name="mixers",
    )(rq, rk, rv, rg, ret_beta, tables["inner_decay"], tables["q_decay"], tables["k_decay"],
      tables["chunk_decay"], aq, ak, av, att_beta)


def _outproj_ffn_kernel(x1_ref, ret_ref, att_ref, wo_ret_ref, wo_att_ref, n_ref, wg_ref, wu_ref, wd_ref,
                        o_ref):
    x2 = (x1_ref[...]
          + jnp.dot(ret_ref[...], wo_ret_ref[...], preferred_element_type=F32)
          + jnp.dot(att_ref[...], wo_att_ref[...], preferred_element_type=F32))
    o_ref[...] = _swiglu_half_step(x2, n_ref[...], wg_ref, wu_ref, wd_ref)


def _outproj_ffn(x1, ret, att, seq, p):
    tokens = x1.shape[0]
    tm = _tiles(seq)["row_tile"]
    row = lambda w: pl.BlockSpec((tm, w), lambda i: (i, 0))
    return pl.pallas_call(
        _outproj_ffn_kernel,
        grid=(tokens // tm,),
        in_specs=[row(D_MODEL), row(RET_WIDTH), row(ATT_WIDTH),
                  _resident((RET_WIDTH, D_MODEL)), _resident((ATT_WIDTH, D_MODEL)),
                  _resident((1, D_MODEL)),
                  _resident((D_MODEL, FFN_DIM)), _resident((D_MODEL, FFN_DIM)),
                  _resident((FFN_DIM, D_MODEL))],
        out_specs=row(D_MODEL),
        out_shape=jax.ShapeDtypeStruct((tokens, D_MODEL), F32),
        compiler_params=_params(),
        name="outproj_ffn2",
    )(x1, ret, att, p["wo_ret"], p["wo_att"], p["n3"], p["wg2"], p["wu2"], p["wd2"])


def _tables(seq):
    pos = jnp.arange(seq, dtype=F32)[:, None]

    def angles(theta, rot_dim):
        half = rot_dim // 2
        inv_freq = theta ** (-jnp.arange(half, dtype=F32) * 2.0 / rot_dim)
        return pos * inv_freq[None, :]

    ang = angles(RET_ROPE_BASE, RET_HEAD_DIM)
    rcos = jnp.concatenate([jnp.cos(ang), jnp.cos(ang)], axis=1)
    rsin = jnp.concatenate([-jnp.sin(ang), jnp.sin(ang)], axis=1)

    ang = angles(ATT_ROPE_BASE, ATT_ROT_DIM)
    half = ATT_ROT_DIM // 2
    pad = jnp.zeros((seq, ATT_HEAD_DIM - ATT_ROT_DIM), F32)
    zero_half = jnp.zeros((seq, half), F32)
    head_cos = jnp.concatenate([jnp.cos(ang), jnp.cos(ang), 1.0 + pad], axis=1)
    head_sin_lo = jnp.concatenate([-jnp.sin(ang), zero_half, pad], axis=1)
    head_sin_hi = jnp.concatenate([zero_half, jnp.sin(ang), pad], axis=1)
    rep = lambda t: jnp.tile(t, (1, HEADS_PER_LANE_TILE))

    c = RET_KERNEL_CHUNK
    log_g = jnp.log(1.0 - 2.0 ** (-5.0 - jnp.arange(RET_HEADS, dtype=F32)))
    i = jnp.arange(c, dtype=F32)
    diff = i[:, None] - i[None, :]
    inner = jnp.where(diff[None] >= 0,
                      jnp.exp(jnp.maximum(diff, 0.0)[None] * log_g[:, None, None]), 0.0)
    lanes = lambda t: jnp.broadcast_to(t[:, :, None], (RET_HEADS, t.shape[1], RET_HEAD_DIM))
    q_decay = lanes(jnp.exp((i[None, :] + 1.0) * log_g[:, None]))
    k_decay = lanes(jnp.exp((c - 1.0 - i[None, :]) * log_g[:, None]))
    chunk_decay = lanes(jnp.exp(c * log_g)[:, None])

    return dict(rcos=rcos, rsin=rsin, acos=rep(head_cos), asin_lo=rep(head_sin_lo),
                asin_hi=rep(head_sin_hi), inner_decay=inner, q_decay=q_decay, k_decay=k_decay,
                chunk_decay=chunk_decay)


def kernel(x, ffn1_norm_w, ffn1_w_gate, ffn1_w_up, ffn1_w_down, mix_norm_w, w_in, ret_out_beta,
           q_norm_w, k_norm_w, att_out_beta, w_out, ffn2_norm_w, ffn2_w_gate, ffn2_w_up, ffn2_w_down):
    b, s, d = x.shape
    assert d == D_MODEL
    tables = _tables(s)
    depth = ffn1_norm_w.shape[0]
    q_logit_scale = ATT_HEAD_DIM ** -0.5 * math.log2(math.e)
    x2d = x.reshape(b * s, d)
    for l in range(depth):
        p = dict(
            n1=ffn1_norm_w[l][None, :], wg1=ffn1_w_gate[l].astype(BF16), wu1=ffn1_w_up[l].astype(BF16),
            wd1=ffn1_w_down[l].astype(BF16), n2=mix_norm_w[l][None, :], w_in=w_in[l].astype(BF16),
            qg=jnp.tile(q_norm_w[l] * q_logit_scale, ATT_HEADS)[None, :],
            kg=jnp.tile(k_norm_w[l], ATT_HEADS)[None, :],
            wo_ret=w_out[l][:RET_WIDTH].astype(BF16), wo_att=w_out[l][RET_WIDTH:].astype(BF16),
            n3=ffn2_norm_w[l][None, :], wg2=ffn2_w_gate[l].astype(BF16), wu2=ffn2_w_up[l].astype(BF16),
            wd2=ffn2_w_down[l].astype(BF16))
        x1, rq, rk, rv, rg, aq, ak, av = _ffn_inproj(x2d, s, p, tables)
        seq3 = lambda t: t.reshape(b, s, t.shape[-1])
        ret, att = _mixers(seq3(rq), seq3(rk), seq3(rv), seq3(rg), ret_out_beta[l][None, :],
                           seq3(aq), seq3(ak), seq3(av), att_out_beta[l][None, :], tables)
        x2d = _outproj_ffn(x1, ret.reshape(b * s, RET_WIDTH), att.reshape(b * s, ATT_WIDTH), s, p)
    return x2d.reshape(b, s, d)
```

```python
import math

import jax
import jax.numpy as jnp
from jax import lax
from jax.experimental import pallas as pl
from jax.experimental.pallas import tpu as pltpu

D_MODEL = 1024
RET_HEADS = 4
RET_HEAD_DIM = 128
RET_WIDTH = RET_HEADS * RET_HEAD_DIM
RET_ROPE_BASE = 10000.0
ATT_HEADS = 8
ATT_HEAD_DIM = 64
ATT_WIDTH = ATT_HEADS * ATT_HEAD_DIM
ATT_ROT_DIM = ATT_HEAD_DIM // 4
ATT_ROPE_BASE = 500000.0
MOBA_BLOCK = 256
MOBA_TOPK = 3
FFN_DIM = 2816
NORM_EPS = 1e-6

V7X_LANES = 128
V7X_MXU_DIM = 256
V7X_VMEM_LIMIT_BYTES = 56 * 1024 * 1024

F32 = jnp.float32
BF16 = jnp.bfloat16
HEADS_PER_LANE_TILE = V7X_LANES // ATT_HEAD_DIM
ATT_LANE_TILES = ATT_WIDTH // V7X_LANES
RET_KERNEL_CHUNK = V7X_MXU_DIM

NT = (((1,), (1,)), ((), ()))
TN = (((0,), (0,)), ((), ()))


def _tiles(seq):
    row_tile = 512 if seq % 512 == 0 else 256
    assert seq % row_tile == 0 and seq % MOBA_BLOCK == 0 and seq % RET_KERNEL_CHUNK == 0
    return dict(row_tile=row_tile)


def _params(dimension_semantics=None):
    return pltpu.CompilerParams(vmem_limit_bytes=V7X_VMEM_LIMIT_BYTES,
                                dimension_semantics=dimension_semantics)


def _resident(shape):
    zeros = (0,) * len(shape)
    return pl.BlockSpec(shape, lambda *_: zeros, pipeline_mode=pl.Buffered(1))


def _rms_rows(x):
    return x * lax.rsqrt(jnp.mean(x * x, axis=-1, keepdims=True) + NORM_EPS)


def _swiglu_activation(h, wg_ref, wu_ref):
    g = jnp.dot(h, wg_ref[...], preferred_element_type=F32)
    u = jnp.dot(h, wu_ref[...], preferred_element_type=F32)
    return (g * jax.nn.sigmoid(g) * u).astype(BF16)


def _swiglu_half_step(x, norm_w, wg_ref, wu_ref, wd_ref):
    h = (_rms_rows(x) * norm_w).astype(BF16)
    a = _swiglu_activation(h, wg_ref, wu_ref)
    return x + 0.5 * jnp.dot(a, wd_ref[...], preferred_element_type=F32)


def _split_bf16(x):
    hi = x.astype(BF16)
    lo = (x - hi.astype(F32)).astype(BF16)
    return hi, lo


def _ffn_inproj_kernel(x_ref, n1_ref, wg_ref, wu_ref, wd_ref, n2_ref, win_ref,
                       rcos_ref, rsin_ref, acos_ref, asin_lo_ref, asin_hi_ref,
                       qg_ref, kg_ref,
                       x1_ref, rq_ref, rk_ref, rv_ref, rg_ref, aq_ref, ak_ref, av_ref):
    x1 = _swiglu_half_step(x_ref[...], n1_ref[...], wg_ref, wu_ref, wd_ref)
    x1_ref[...] = x1
    hn = (_rms_rows(x1) * n2_ref[...]).astype(BF16)

    def proj(col0, width):
        return jnp.dot(hn, win_ref[:, col0:col0 + width], preferred_element_type=F32)

    R, A, L = RET_WIDTH, ATT_WIDTH, V7X_LANES
    rcos, rsin = rcos_ref[...], rsin_ref[...]

    def ret_rope(p, out_ref, scale):
        for h in range(RET_HEADS):
            t = p[:, h * L:(h + 1) * L]
            r = t * rcos + pltpu.roll(t, RET_HEAD_DIM // 2, 1) * rsin
            if scale is not None:
                r = r * scale
            out_ref[:, h * L:(h + 1) * L] = r.astype(out_ref.dtype)

    acos, asin_lo, asin_hi = acos_ref[...], asin_lo_ref[...], asin_hi_ref[...]
    half_rot = ATT_ROT_DIM // 2

    first_head = lax.broadcasted_iota(jnp.int32, (x1.shape[0], L), 1) < ATT_HEAD_DIM

    def att_norm_rope(p, gain_ref, out_ref):
        for c in range(A // L):
            t = p[:, c * L:(c + 1) * L]
            sq = t * t
            ss = jnp.where(first_head,
                           jnp.sum(jnp.where(first_head, sq, 0.0), axis=-1, keepdims=True),
                           jnp.sum(jnp.where(first_head, 0.0, sq), axis=-1, keepdims=True))
            u = t * lax.rsqrt(ss * (1.0 / ATT_HEAD_DIM) + NORM_EPS) * gain_ref[:, c * L:(c + 1) * L]
            r = (u * acos + pltpu.roll(u, half_rot, 1) * asin_hi
                 + pltpu.roll(u, L - half_rot, 1) * asin_lo)
            out_ref[:, c * L:(c + 1) * L] = r.astype(out_ref.dtype)

    o0 = 4 * R
    att_norm_rope(proj(o0, A), qg_ref, aq_ref)
    att_norm_rope(proj(o0 + A, A), kg_ref, ak_ref)
    ret_rope(proj(0, R), rq_ref, None)
    ret_rope(proj(R, R), rk_ref, RET_HEAD_DIM ** -0.5)
    rv_ref[...] = proj(2 * R, R).astype(rv_ref.dtype)
    rg_ref[...] = proj(3 * R, R).astype(rg_ref.dtype)
    av_ref[...] = proj(o0 + 2 * A, A).astype(av_ref.dtype)


def _ffn_inproj(x2d, seq, p, tables):
    tokens = x2d.shape[0]
    tm = _tiles(seq)["row_tile"]
    pos_blocks = seq // tm
    row = lambda w: pl.BlockSpec((tm, w), lambda i: (i, 0))
    pos = lambda: pl.BlockSpec((tm, V7X_LANES), lambda i: (i % pos_blocks, 0))
    in_width = p["w_in"].shape[1]
    out_shapes = ([jax.ShapeDtypeStruct((tokens, D_MODEL), F32)]
                  + [jax.ShapeDtypeStruct((tokens, RET_WIDTH), BF16)] * 3
                  + [jax.ShapeDtypeStruct((tokens, RET_WIDTH), F32)]
                  + [jax.ShapeDtypeStruct((tokens, ATT_WIDTH), BF16)] * 3)
    return pl.pallas_call(
        _ffn_inproj_kernel,
        grid=(tokens // tm,),
        in_specs=[row(D_MODEL), _resident((1, D_MODEL)),
                  _resident((D_MODEL, FFN_DIM)), _resident((D_MODEL, FFN_DIM)),
                  _resident((FFN_DIM, D_MODEL)), _resident((1, D_MODEL)),
                  _resident((D_MODEL, in_width)),
                  pos(), pos(), pos(), pos(), pos(),
                  _resident((1, ATT_WIDTH)), _resident((1, ATT_WIDTH))],
        out_specs=[row(D_MODEL)] + [row(RET_WIDTH)] * 4 + [row(ATT_WIDTH)] * 3,
        out_shape=out_shapes,
        compiler_params=_params(),
        name="ffn1_inproj",
    )(x2d, p["n1"], p["wg1"], p["wu1"], p["wd1"], p["n2"], p["w_in"],
      tables["rcos"], tables["rsin"], tables["acos"], tables["asin_lo"], tables["asin_hi"],
      p["qg"], p["kg"])


def _retention_stream(q_ref, k_ref, v_ref, g_ref, beta_ref, inner_ref, qdec_ref, kdec_ref, cdec_ref,
                      o_ref):
    c = RET_KERNEL_CHUNK
    n_chunks = q_ref.shape[1] // c
    rows = lambda i: slice(i * c, (i + 1) * c)
    live = {"state": None}

    def begin(i):
        live["scores", i] = lax.dot_general(q_ref[0, rows(i), :], k_ref[0, rows(i), :], NT,
                                            preferred_element_type=F32) * inner_ref[0]
        if i < n_chunks - 1:
            kd = (k_ref[0, rows(i), :].astype(F32) * kdec_ref[0]).astype(BF16)
            live["increment", i] = lax.dot_general(kd, v_ref[0, rows(i), :], TN,
                                                   preferred_element_type=F32)

    def finish(i):
        o = jnp.dot(live.pop(("scores", i)).astype(BF16), v_ref[0, rows(i), :],
                    preferred_element_type=F32)
        state = live["state"]
        if state is not None:
            o = o + jnp.dot(q_ref[0, rows(i), :], state.astype(BF16),
                            preferred_element_type=F32) * qdec_ref[0]
        if i < n_chunks - 1:
            inc = live.pop(("increment", i))
            live["state"] = inc if state is None else state * cdec_ref[0] + inc
        g = g_ref[0, rows(i), :]
        o_ref[0, rows(i), :] = (_rms_rows(o) * (g * jax.nn.sigmoid(g)) * beta_ref[...]).astype(o_ref.dtype)

    return begin, finish


ONES_ROWS = 16
STAT_ROWS = 8
KMEAN_ROWS = 16
SCORE_LOOKAHEAD = 6
PV_DELAY = 0
GROUPS_PER_STEP = 2
RET_CHAIN_STRIDE = 8
RET_FINISH_AFTER = 3


def _mixer_kernel(rq_ref, rk_ref, rv_ref, rg_ref, rbeta_ref, inner_ref, qdec_ref, kdec_ref, cdec_ref,
                  q_ref, k_ref, v_ref, beta_ref, ret_ref, o_ref, vt_ref):
    bs = MOBA_BLOCK
    seq = k_ref.shape[1]
    nb = seq // bs
    dh = ATT_HEAD_DIM
    L = V7X_LANES
    heads = range(HEADS_PER_LANE_TILE)
    groups = range(GROUPS_PER_STEP)
    lanes = lambda g: slice(g * L, (g + 1) * L)

    ret_hooks = [_retention_stream(rq_ref.at[:, :, lanes(g)], rk_ref.at[:, :, lanes(g)],
                                   rv_ref.at[:, :, lanes(g)], rg_ref.at[:, :, lanes(g)],
                                   rbeta_ref.at[:, lanes(g)], inner_ref.at[g:g + 1],
                                   qdec_ref.at[g:g + 1], kdec_ref.at[g:g + 1], cdec_ref.at[g:g + 1],
                                   ret_ref.at[:, :, lanes(g)]) for g in groups]

    km_hi, km_lo = {}, {}
    for g in groups:
        vt = jnp.transpose(v_ref[0, :, lanes(g)].astype(F32))
        for h in heads:
            for n in range(nb):
                vt_ref[g, h, n, :dh, :] = vt[h * dh:(h + 1) * dh, n * bs:(n + 1) * bs].astype(BF16)
                vt_ref[g, h, n, dh:, :] = jnp.ones((ONES_ROWS, bs), BF16)
        kmean = jnp.mean(k_ref[0, :, lanes(g)].astype(F32).reshape(nb, bs, L), axis=1)
        kmean = jnp.concatenate([kmean, jnp.zeros((KMEAN_ROWS - nb, L), F32)], axis=0)
        km_hi[g], km_lo[g] = _split_bf16(kmean)

    lane = lax.broadcasted_iota(jnp.int32, (bs, L), 1)
    head_lanes = [(lane >= h * dh) & (lane < (h + 1) * dh) for h in heads]
    row_id = lax.broadcasted_iota(jnp.int32, (KMEAN_ROWS, bs), 0)
    causal_t = (lax.broadcasted_iota(jnp.int32, (bs, bs), 0)
                <= lax.broadcasted_iota(jnp.int32, (bs, bs), 1))
    stat = lambda t: jnp.broadcast_to(t, (STAT_ROWS, bs))
    wide = lambda t: jnp.broadcast_to(t[:1], (dh, bs))

    masked_q = {}

    def query(g, j, h):
        if (g, j, h) not in masked_q:
            q = q_ref[0, j * bs:(j + 1) * bs, lanes(g)]
            masked_q[g, j, h] = jnp.where(head_lanes[h], q, jnp.zeros_like(q))
        return masked_q[g, j, h]

    def combine(g, j, h, pv):
        sel = [None] * j
        if j > MOBA_TOPK:
            gate = (lax.dot_general(km_hi[g], query(g, j, h), NT, preferred_element_type=F32)
                    + lax.dot_general(km_lo[g], query(g, j, h), NT, preferred_element_type=F32))
            gate = jnp.where(row_id < j, gate, -jnp.inf)
            rank = jnp.zeros(gate.shape, jnp.int32)
            for m in range(j):
                other = gate[m:m + 1, :]
                ahead = (other > gate) | ((other == gate) & (m < row_id))
                rank = rank + ahead.astype(jnp.int32)
            keep = (rank < MOBA_TOPK).astype(F32)
            sel = [stat(keep[n:n + 1, :]) > 0.5 for n in range(j)]
        m_all = pv[j][0]
        for n in range(j):
            m_n = pv[n][0]
            m_all = jnp.maximum(m_all, m_n if sel[n] is None else jnp.where(sel[n], m_n, -jnp.inf))
        l_sum = jnp.zeros((STAT_ROWS, bs), F32)
        acc = jnp.zeros((dh, bs), F32)
        for n in range(j + 1):
            m_n, ov = pv[n]
            w = jnp.exp2(m_n - m_all)
            if n < j and sel[n] is not None:
                w = jnp.where(sel[n], w, 0.0)
            l_sum = l_sum + w * ov[dh:dh + STAT_ROWS, :]
            acc = acc + wide(w) * ov[:dh, :]
        res = acc * wide(1.0 / l_sum)
        ms = jnp.mean(res * res, axis=0, keepdims=True)
        return res * lax.rsqrt(ms + NORM_EPS)

    group_work = [(j, n, h) for j in reversed(range(nb)) for n in range(j + 1) for h in heads]
    work = [(g,) + w for g in groups for w in group_work]
    assert RET_CHAIN_STRIDE * (nb - 1) + RET_FINISH_AFTER < len(group_work)

    def retention_hook(kind, chain):
        g, since = divmod(chain, len(group_work))
        since -= kind * RET_FINISH_AFTER
        if since >= 0 and since % RET_CHAIN_STRIDE == 0 and since // RET_CHAIN_STRIDE < nb:
            ret_hooks[g][kind](since // RET_CHAIN_STRIDE)

    scores, probs, pv, normed = {}, {}, {}, {}
    for step in range(-SCORE_LOOKAHEAD, len(work) + PV_DELAY):
        if step + SCORE_LOOKAHEAD < len(work):
            g, j, n, h = work[step + SCORE_LOOKAHEAD]
            scores[g, j, n, h] = lax.dot_general(k_ref[0, n * bs:(n + 1) * bs, lanes(g)],
                                                 query(g, j, h), NT,
                                                 preferred_element_type=F32)
        if 0 <= step < len(work):
            g, j, n, h = work[step]
            retention_hook(0, step)
            st = scores.pop((g, j, n, h))
            if n == j:
                st = jnp.where(causal_t, st, -jnp.inf)
            mt = jnp.max(st, axis=0, keepdims=True)
            probs[g, j, n, h] = (stat(mt), jnp.exp2(st - mt).astype(BF16))
        if 0 <= step - PV_DELAY < len(work):
            g, j, n, h = work[step - PV_DELAY]
            mt, pt = probs.pop((g, j, n, h))
            pv.setdefault((g, j, h), {})[n] = (mt, jnp.dot(vt_ref[g, h, n], pt,
                                                           preferred_element_type=F32))
            retention_hook(1, step - PV_DELAY)
            if n == j:
                normed[g, j, h] = combine(g, j, h, pv.pop((g, j, h)))
                if h == heads[-1]:
                    att = jnp.transpose(jnp.concatenate([normed.pop((g, j, hh)) for hh in heads],
                                                        axis=0))
                    o_ref[0, j * bs:(j + 1) * bs, lanes(g)] = (
                        att * beta_ref[:, lanes(g)]).astype(o_ref.dtype)


def _mixers(rq, rk, rv, rg, ret_beta, aq, ak, av, att_beta, tables):
    b, s, _ = aq.shape
    L = V7X_LANES
    nb = s // MOBA_BLOCK
    c = RET_KERNEL_CHUNK
    assert nb <= KMEAN_ROWS
    assert RET_HEADS == ATT_LANE_TILES and RET_HEAD_DIM == L and c == MOBA_BLOCK
    gps = GROUPS_PER_STEP
    assert ATT_LANE_TILES % gps == 0
    cols = lambda: pl.BlockSpec((1, s, gps * L), lambda bi, t: (bi, 0, t))
    gain = lambda: pl.BlockSpec((1, gps * L), lambda bi, t: (0, t))
    per_head = lambda r, w: pl.BlockSpec((gps, r, w), lambda bi, t: (t, 0, 0))
    return pl.pallas_call(
        _mixer_kernel,
        grid=(b, ATT_LANE_TILES // gps),
        in_specs=[cols(), cols(), cols(), cols(), gain(),
                  per_head(c, c), per_head(c, L), per_head(c, L), per_head(1, L),
                  cols(), cols(), cols(), gain()],
        out_specs=[cols(), cols()],
        out_shape=[jax.ShapeDtypeStruct((b, s, RET_WIDTH), BF16),
                   jax.ShapeDtypeStruct((b, s, ATT_WIDTH), BF16)],
        scratch_shapes=[pltpu.VMEM((gps, HEADS_PER_LANE_TILE, nb, ATT_HEAD_DIM + ONES_ROWS, MOBA_BLOCK),
                                   BF16)],
        compiler_params=_params(),
        name="mixers",
    )(rq, rk, rv, rg, ret_beta, tables["inner_decay"], tables["q_decay"], tables["k_decay"],
      tables["chunk_decay"], aq, ak, av, att_beta)


def _outproj_ffn_kernel(x1_ref, ret_ref, att_ref, wo_ret_ref, wo_att_ref, n_ref, wg_ref, wu_ref, wd_ref,
                        o_ref):
    x2 = (x1_ref[...]
          + jnp.dot(ret_ref[...], wo_ret_ref[...], preferred_element_type=F32)
          + jnp.dot(att_ref[...], wo_att_ref[...], preferred_element_type=F32))
    o_ref[...] = _swiglu_half_step(x2, n_ref[...], wg_ref, wu_ref, wd_ref)


def _outproj_ffn(x1, ret, att, seq, p):
    tokens = x1.shape[0]
    tm = _tiles(seq)["row_tile"]
    row = lambda w: pl.BlockSpec((tm, w), lambda i: (i, 0))
    return pl.pallas_call(
        _outproj_ffn_kernel,
        grid=(tokens // tm,),
        in_specs=[row(D_MODEL), row(RET_WIDTH), row(ATT_WIDTH),
                  _resident((RET_WIDTH, D_MODEL)), _resident((ATT_WIDTH, D_MODEL)),
                  _resident((1, D_MODEL)),
                  _resident((D_MODEL, FFN_DIM)), _resident((D_MODEL, FFN_DIM)),
                  _resident((FFN_DIM, D_MODEL))],
        out_specs=row(D_MODEL),
        out_shape=jax.ShapeDtypeStruct((tokens, D_MODEL), F32),
        compiler_params=_params(),
        name="outproj_ffn2",
    )(x1, ret, att, p["wo_ret"], p["wo_att"], p["n3"], p["wg2"], p["wu2"], p["wd2"])


def _tables(seq):
    pos = jnp.arange(seq, dtype=F32)[:, None]

    def angles(theta, rot_dim):
        half = rot_dim // 2
        inv_freq = theta ** (-jnp.arange(half, dtype=F32) * 2.0 / rot_dim)
        return pos * inv_freq[None, :]

    ang = angles(RET_ROPE_BASE, RET_HEAD_DIM)
    rcos = jnp.concatenate([jnp.cos(ang), jnp.cos(ang)], axis=1)
    rsin = jnp.concatenate([-jnp.sin(ang), jnp.sin(ang)], axis=1)

    ang = angles(ATT_ROPE_BASE, ATT_ROT_DIM)
    half = ATT_ROT_DIM // 2
    pad = jnp.zeros((seq, ATT_HEAD_DIM - ATT_ROT_DIM), F32)
    zero_half = jnp.zeros((seq, half), F32)
    head_cos = jnp.concatenate([jnp.cos(ang), jnp.cos(ang), 1.0 + pad], axis=1)
    head_sin_lo = jnp.concatenate([-jnp.sin(ang), zero_half, pad], axis=1)
    head_sin_hi = jnp.concatenate([zero_half, jnp.sin(ang), pad], axis=1)
    rep = lambda t: jnp.tile(t, (1, HEADS_PER_LANE_TILE))

    c = RET_KERNEL_CHUNK
    log_g = jnp.log(1.0 - 2.0 ** (-5.0 - jnp.arange(RET_HEADS, dtype=F32)))
    i = jnp.arange(c, dtype=F32)
    diff = i[:, None] - i[None, :]
    inner = jnp.where(diff[None] >= 0,
                      jnp.exp(jnp.maximum(diff, 0.0)[None] * log_g[:, None, None]), 0.0)
    lanes = lambda t: jnp.broadcast_to(t[:, :, None], (RET_HEADS, t.shape[1], RET_HEAD_DIM))
    q_decay = lanes(jnp.exp((i[None, :] + 1.0) * log_g[:, None]))
    k_decay = lanes(jnp.exp((c - 1.0 - i[None, :]) * log_g[:, None]))
    chunk_decay = lanes(jnp.exp(c * log_g)[:, None])

    return dict(rcos=rcos, rsin=rsin, acos=rep(head_cos), asin_lo=rep(head_sin_lo),
                asin_hi=rep(head_sin_hi), inner_decay=inner, q_decay=q_decay, k_decay=k_decay,
                chunk_decay=chunk_decay)


def kernel(x, ffn1_norm_w, ffn1_w_gate, ffn1_w_up, ffn1_w_down, mix_norm_w, w_in, ret_out_beta,
           q_norm_w, k_norm_w, att_out_beta, w_out, ffn2_norm_w, ffn2_w_gate, ffn2_w_up, ffn2_w_down):
    b, s, d = x.shape
    assert d == D_MODEL
    tables = _tables(s)
    depth = ffn1_norm_w.shape[0]
    q_logit_scale = ATT_HEAD_DIM ** -0.5 * math.log2(math.e)
    x2d = x.reshape(b * s, d)
    for l in range(depth):
        p = dict(
            n1=ffn1_norm_w[l][None, :], wg1=ffn1_w_gate[l].astype(BF16), wu1=ffn1_w_up[l].astype(BF16),
            wd1=ffn1_w_down[l].astype(BF16), n2=mix_norm_w[l][None, :], w_in=w_in[l].astype(BF16),
            qg=jnp.tile(q_norm_w[l] * q_logit_scale, ATT_HEADS)[None, :],
            kg=jnp.tile(k_norm_w[l], ATT_HEADS)[None, :],
            wo_ret=w_out[l][:RET_WIDTH].astype(BF16), wo_att=w_out[l][RET_WIDTH:].astype(BF16),
            n3=ffn2_norm_w[l][None, :], wg2=ffn2_w_gate[l].astype(BF16), wu2=ffn2_w_up[l].astype(BF16),
            wd2=ffn2_w_down[l].astype(BF16))
        x1, rq, rk, rv, rg, aq, ak, av = _ffn_inproj(x2d, s, p, tables)
        seq3 = lambda t: t.reshape(b, s, t.shape[-1])
        ret, att = _mixers(seq3(rq), seq3(rk), seq3(rv), seq3(rg), ret_out_beta[l][None, :],
                           seq3(aq), seq3(ak), seq3(av), att_out_beta[l][None, :], tables)
        x2d = _outproj_ffn(x1, ret.reshape(b * s, RET_WIDTH), att.reshape(b * s, ATT_WIDTH), s, p)
    return x2d.reshape(b, s, d)
```

```python
import math

import jax
import jax.numpy as jnp
from jax import lax
from jax.experimental import pallas as pl
from jax.experimental.pallas import tpu as pltpu

D_MODEL = 1024
RET_HEADS = 4
RET_HEAD_DIM = 128
RET_WIDTH = RET_HEADS * RET_HEAD_DIM
RET_ROPE_BASE = 10000.0
ATT_HEADS = 8
ATT_HEAD_DIM = 64
ATT_WIDTH = ATT_HEADS * ATT_HEAD_DIM
ATT_ROT_DIM = ATT_HEAD_DIM // 4
ATT_ROPE_BASE = 500000.0
MOBA_BLOCK = 256
MOBA_TOPK = 3
FFN_DIM = 2816
NORM_EPS = 1e-6

V7X_LANES = 128
V7X_MXU_DIM = 256
V7X_VMEM_LIMIT_BYTES = 56 * 1024 * 1024

F32 = jnp.float32
BF16 = jnp.bfloat16
HEADS_PER_LANE_TILE = V7X_LANES // ATT_HEAD_DIM
ATT_LANE_TILES = ATT_WIDTH // V7X_LANES
RET_KERNEL_CHUNK = V7X_MXU_DIM

NT = (((1,), (1,)), ((), ()))
TN = (((0,), (0,)), ((), ()))


def _tiles(seq):
    row_tile = 512 if seq % 512 == 0 else 256
    assert seq % row_tile == 0 and seq % MOBA_BLOCK == 0 and seq % RET_KERNEL_CHUNK == 0
    return dict(row_tile=row_tile)


def _params(dimension_semantics=None):
    return pltpu.CompilerParams(vmem_limit_bytes=V7X_VMEM_LIMIT_BYTES,
                                dimension_semantics=dimension_semantics)


def _resident(shape):
    zeros = (0,) * len(shape)
    return pl.BlockSpec(shape, lambda *_: zeros, pipeline_mode=pl.Buffered(1))


def _rms_rows(x):
    return x * lax.rsqrt(jnp.mean(x * x, axis=-1, keepdims=True) + NORM_EPS)


def _swiglu_activation(h, wg_ref, wu_ref):
    g = jnp.dot(h, wg_ref[...], preferred_element_type=F32)
    u = jnp.dot(h, wu_ref[...], preferred_element_type=F32)
    return (g * jax.nn.sigmoid(g) * u).astype(BF16)


def _swiglu_half_step(x, norm_w, wg_ref, wu_ref, wd_ref):
    h = (_rms_rows(x) * norm_w).astype(BF16)
    a = _swiglu_activation(h, wg_ref, wu_ref)
    return x + 0.5 * jnp.dot(a, wd_ref[...], preferred_element_type=F32)


def _split_bf16(x):
    hi = x.astype(BF16)
    lo = (x - hi.astype(F32)).astype(BF16)
    return hi, lo


def _ffn_inproj_kernel(x_ref, n1_ref, wg_ref, wu_ref, wd_ref, n2_ref, win_ref,
                       rcos_ref, rsin_ref, acos_ref, asin_lo_ref, asin_hi_ref,
                       qg_ref, kg_ref,
                       x1_ref, rq_ref, rk_ref, rv_ref, rg_ref, aq_ref, ak_ref, av_ref):
    R, A, L = RET_WIDTH, ATT_WIDTH, V7X_LANES
    half = x_ref.shape[0] // 2
    halves = (slice(0, half), slice(half, 2 * half))

    h = [(_rms_rows(x_ref[r, :]) * n1_ref[...]).astype(BF16) for r in halves]
    act = [_swiglu_activation(t, wg_ref, wu_ref) for t in h]
    x1 = [x_ref[r, :] + 0.5 * jnp.dot(a, wd_ref[...], preferred_element_type=F32)
          for r, a in zip(halves, act)]
    for r, t in zip(halves, x1):
        x1_ref[r, :] = t
    hn = [(_rms_rows(t) * n2_ref[...]).astype(BF16) for t in x1]

    half_rot = ATT_ROT_DIM // 2
    first_head = lax.broadcasted_iota(jnp.int32, (half, L), 1) < ATT_HEAD_DIM

    def project(r, hn_r):
        def proj(col0, width):
            return jnp.dot(hn_r, win_ref[:, col0:col0 + width], preferred_element_type=F32)

        def ret_rope(p, out_ref, scale):
            for hd in range(RET_HEADS):
                t = p[:, hd * L:(hd + 1) * L]
                o = t * rcos_ref[r, :] + pltpu.roll(t, RET_HEAD_DIM // 2, 1) * rsin_ref[r, :]
                if scale is not None:
                    o = o * scale
                out_ref[r, hd * L:(hd + 1) * L] = o.astype(out_ref.dtype)

        def att_norm_rope(p, gain_ref, out_ref):
            for c in range(A // L):
                t = p[:, c * L:(c + 1) * L]
                sq = t * t
                ss = jnp.where(first_head,
                               jnp.sum(jnp.where(first_head, sq, 0.0), axis=-1, keepdims=True),
                               jnp.sum(jnp.where(first_head, 0.0, sq), axis=-1, keepdims=True))
                u = (t * lax.rsqrt(ss * (1.0 / ATT_HEAD_DIM) + NORM_EPS)
                     * gain_ref[:, c * L:(c + 1) * L])
                o = (u * acos_ref[r, :] + pltpu.roll(u, half_rot, 1) * asin_hi_ref[r, :]
                     + pltpu.roll(u, L - half_rot, 1) * asin_lo_ref[r, :])
                out_ref[r, c * L:(c + 1) * L] = o.astype(out_ref.dtype)

        o0 = 4 * R
        att_norm_rope(proj(o0, A), qg_ref, aq_ref)
        att_norm_rope(proj(o0 + A, A), kg_ref, ak_ref)
        ret_rope(proj(0, R), rq_ref, None)
        ret_rope(proj(R, R), rk_ref, RET_HEAD_DIM ** -0.5)
        rv_ref[r, :] = proj(2 * R, R).astype(rv_ref.dtype)
        rg_ref[r, :] = proj(3 * R, R).astype(rg_ref.dtype)
        av_ref[r, :] = proj(o0 + 2 * A, A).astype(av_ref.dtype)

    for r, t in zip(halves, hn):
        project(r, t)


def _ffn_inproj(x2d, seq, p, tables):
    tokens = x2d.shape[0]
    tm = _tiles(seq)["row_tile"]
    pos_blocks = seq // tm
    row = lambda w: pl.BlockSpec((tm, w), lambda i: (i, 0))
    pos = lambda: pl.BlockSpec((tm, V7X_LANES), lambda i: (i % pos_blocks, 0))
    in_width = p["w_in"].shape[1]
    out_shapes = ([jax.ShapeDtypeStruct((tokens, D_MODEL), F32)]
                  + [jax.ShapeDtypeStruct((tokens, RET_WIDTH), BF16)] * 3
                  + [jax.ShapeDtypeStruct((tokens, RET_WIDTH), F32)]
                  + [jax.ShapeDtypeStruct((tokens, ATT_WIDTH), BF16)] * 3)
    return pl.pallas_call(
        _ffn_inproj_kernel,
        grid=(tokens // tm,),
        in_specs=[row(D_MODEL), _resident((1, D_MODEL)),
                  _resident((D_MODEL, FFN_DIM)), _resident((D_MODEL, FFN_DIM)),
                  _resident((FFN_DIM, D_MODEL)), _resident((1, D_MODEL)),
                  _resident((D_MODEL, in_width)),
                  pos(), pos(), pos(), pos(), pos(),
                  _resident((1, ATT_WIDTH)), _resident((1, ATT_WIDTH))],
        out_specs=[row(D_MODEL)] + [row(RET_WIDTH)] * 4 + [row(ATT_WIDTH)] * 3,
        out_shape=out_shapes,
        compiler_params=_params(),
        name="ffn1_inproj",
    )(x2d, p["n1"], p["wg1"], p["wu1"], p["wd1"], p["n2"], p["w_in"],
      tables["rcos"], tables["rsin"], tables["acos"], tables["asin_lo"], tables["asin_hi"],
      p["qg"], p["kg"])


def _retention_stream(q_ref, k_ref, v_ref, g_ref, beta_ref, inner_ref, qdec_ref, kdec_ref, cdec_ref,
                      o_ref):
    c = RET_KERNEL_CHUNK
    n_chunks = q_ref.shape[1] // c
    rows = lambda i: slice(i * c, (i + 1) * c)
    live = {"state": None}

    def begin(i):
        live["scores", i] = lax.dot_general(q_ref[0, rows(i), :], k_ref[0, rows(i), :], NT,
                                            preferred_element_type=F32) * inner_ref[0]
        if i < n_chunks - 1:
            kd = (k_ref[0, rows(i), :].astype(F32) * kdec_ref[0]).astype(BF16)
            live["increment", i] = lax.dot_general(kd, v_ref[0, rows(i), :], TN,
                                                   preferred_element_type=F32)

    def finish(i):
        o = jnp.dot(live.pop(("scores", i)).astype(BF16), v_ref[0, rows(i), :],
                    preferred_element_type=F32)
        state = live["state"]
        if state is not None:
            o = o + jnp.dot(q_ref[0, rows(i), :], state.astype(BF16),
                            preferred_element_type=F32) * qdec_ref[0]
        if i < n_chunks - 1:
            inc = live.pop(("increment", i))
            live["state"] = inc if state is None else state * cdec_ref[0] + inc
        g = g_ref[0, rows(i), :]
        o_ref[0, rows(i), :] = (_rms_rows(o) * (g * jax.nn.sigmoid(g)) * beta_ref[...]).astype(o_ref.dtype)

    return begin, finish


ONES_ROWS = 16
STAT_ROWS = 8
KMEAN_ROWS = 16
SCORE_LOOKAHEAD = 6
PV_DELAY = 0
GROUPS_PER_STEP = 2
RET_CHAIN_STRIDE = 8
RET_FINISH_AFTER = 3


def _mixer_kernel(rq_ref, rk_ref, rv_ref, rg_ref, rbeta_ref, inner_ref, qdec_ref, kdec_ref, cdec_ref,
                  q_ref, k_ref, v_ref, beta_ref, ret_ref, o_ref, vt_ref):
    bs = MOBA_BLOCK
    seq = k_ref.shape[1]
    nb = seq // bs
    dh = ATT_HEAD_DIM
    L = V7X_LANES
    heads = range(HEADS_PER_LANE_TILE)
    groups = range(GROUPS_PER_STEP)
    lanes = lambda g: slice(g * L, (g + 1) * L)

    ret_hooks = [_retention_stream(rq_ref.at[:, :, lanes(g)], rk_ref.at[:, :, lanes(g)],
                                   rv_ref.at[:, :, lanes(g)], rg_ref.at[:, :, lanes(g)],
                                   rbeta_ref.at[:, lanes(g)], inner_ref.at[g:g + 1],
                                   qdec_ref.at[g:g + 1], kdec_ref.at[g:g + 1], cdec_ref.at[g:g + 1],
                                   ret_ref.at[:, :, lanes(g)]) for g in groups]

    km_hi, km_lo = {}, {}
    for g in groups:
        vt = jnp.transpose(v_ref[0, :, lanes(g)].astype(F32))
        for h in heads:
            for n in range(nb):
                vt_ref[g, h, n, :dh, :] = vt[h * dh:(h + 1) * dh, n * bs:(n + 1) * bs].astype(BF16)
                vt_ref[g, h, n, dh:, :] = jnp.ones((ONES_ROWS, bs), BF16)
        kmean = jnp.mean(k_ref[0, :, lanes(g)].astype(F32).reshape(nb, bs, L), axis=1)
        kmean = jnp.concatenate([kmean, jnp.zeros((KMEAN_ROWS - nb, L), F32)], axis=0)
        km_hi[g], km_lo[g] = _split_bf16(kmean)

    lane = lax.broadcasted_iota(jnp.int32, (bs, L), 1)
    head_lanes = [(lane >= h * dh) & (lane < (h + 1) * dh) for h in heads]
    row_id = lax.broadcasted_iota(jnp.int32, (KMEAN_ROWS, bs), 0)
    causal_t = (lax.broadcasted_iota(jnp.int32, (bs, bs), 0)
                <= lax.broadcasted_iota(jnp.int32, (bs, bs), 1))
    stat = lambda t: jnp.broadcast_to(t, (STAT_ROWS, bs))
    wide = lambda t: jnp.broadcast_to(t[:1], (dh, bs))

    masked_q = {}

    def query(g, j, h):
        if (g, j, h) not in masked_q:
            q = q_ref[0, j * bs:(j + 1) * bs, lanes(g)]
            masked_q[g, j, h] = jnp.where(head_lanes[h], q, jnp.zeros_like(q))
        return masked_q[g, j, h]

    def combine(g, j, h, pv):
        sel = [None] * j
        if j > MOBA_TOPK:
            gate = (lax.dot_general(km_hi[g], query(g, j, h), NT, preferred_element_type=F32)
                    + lax.dot_general(km_lo[g], query(g, j, h), NT, preferred_element_type=F32))
            gate = jnp.where(row_id < j, gate, -jnp.inf)
            rank = jnp.zeros(gate.shape, jnp.int32)
            for m in range(j):
                other = gate[m:m + 1, :]
                ahead = (other > gate) | ((other == gate) & (m < row_id))
                rank = rank + ahead.astype(jnp.int32)
            keep = (rank < MOBA_TOPK).astype(F32)
            sel = [stat(keep[n:n + 1, :]) > 0.5 for n in range(j)]
        m_all = pv[j][0]
        for n in range(j):
            m_n = pv[n][0]
            m_all = jnp.maximum(m_all, m_n if sel[n] is None else jnp.where(sel[n], m_n, -jnp.inf))
        l_sum = jnp.zeros((STAT_ROWS, bs), F32)
        acc = jnp.zeros((dh, bs), F32)
        for n in range(j + 1):
            m_n, ov = pv[n]
            w = jnp.exp2(m_n - m_all)
            if n < j and sel[n] is not None:
                w = jnp.where(sel[n], w, 0.0)
            l_sum = l_sum + w * ov[dh:dh + STAT_ROWS, :]
            acc = acc + wide(w) * ov[:dh, :]
        res = acc * wide(1.0 / l_sum)
        ms = jnp.mean(res * res, axis=0, keepdims=True)
        return res * lax.rsqrt(ms + NORM_EPS)

    group_work = [(j, n, h) for j in reversed(range(nb)) for n in range(j + 1) for h in heads]
    work = [(g,) + w for g in groups for w in group_work]
    assert RET_CHAIN_STRIDE * (nb - 1) + RET_FINISH_AFTER < len(group_work)

    def retention_hook(kind, chain):
        g, since = divmod(chain, len(group_work))
        since -= kind * RET_FINISH_AFTER
        if since >= 0 and since % RET_CHAIN_STRIDE == 0 and since // RET_CHAIN_STRIDE < nb:
            ret_hooks[g][kind](since // RET_CHAIN_STRIDE)

    scores, probs, pv, normed = {}, {}, {}, {}
    for step in range(-SCORE_LOOKAHEAD, len(work) + PV_DELAY):
        if step + SCORE_LOOKAHEAD < len(work):
            g, j, n, h = work[step + SCORE_LOOKAHEAD]
            scores[g, j, n, h] = lax.dot_general(k_ref[0, n * bs:(n + 1) * bs, lanes(g)],
                                                 query(g, j, h), NT,
                                                 preferred_element_type=F32)
        if 0 <= step < len(work):
            g, j, n, h = work[step]
            retention_hook(0, step)
            st = scores.pop((g, j, n, h))
            if n == j:
                st = jnp.where(causal_t, st, -jnp.inf)
            mt = jnp.max(st, axis=0, keepdims=True)
            probs[g, j, n, h] = (stat(mt), jnp.exp2(st - mt).astype(BF16))
        if 0 <= step - PV_DELAY < len(work):
            g, j, n, h = work[step - PV_DELAY]
            mt, pt = probs.pop((g, j, n, h))
            pv.setdefault((g, j, h), {})[n] = (mt, jnp.dot(vt_ref[g, h, n], pt,
                                                           preferred_element_type=F32))
            retention_hook(1, step - PV_DELAY)
            if n == j:
                normed[g, j, h] = combine(g, j, h, pv.pop((g, j, h)))
                if h == heads[-1]:
                    att = jnp.transpose(jnp.concatenate([normed.pop((g, j, hh)) for hh in heads],
                                                        axis=0))
                    o_ref[0, j * bs:(j + 1) * bs, lanes(g)] = (
                        att * beta_ref[:, lanes(g)]).astype(o_ref.dtype)


def _mixers(rq, rk, rv, rg, ret_beta, aq, ak, av, att_beta, tables):
    b, s, _ = aq.shape
    L = V7X_LANES
    nb = s // MOBA_BLOCK
    c = RET_KERNEL_CHUNK
    assert nb <= KMEAN_ROWS
    assert RET_HEADS == ATT_LANE_TILES and RET_HEAD_DIM == L and c == MOBA_BLOCK
    gps = GROUPS_PER_STEP
    assert ATT_LANE_TILES % gps == 0
    cols = lambda: pl.BlockSpec((1, s, gps * L), lambda bi, t: (bi, 0, t))
    gain = lambda: pl.BlockSpec((1, gps * L), lambda bi, t: (0, t))
    per_head = lambda r, w: pl.BlockSpec((gps, r, w), lambda bi, t: (t, 0, 0))
    return pl.pallas_call(
        _mixer_kernel,
        grid=(b, ATT_LANE_TILES // gps),
        in_specs=[cols(), cols(), cols(), cols(), gain(),
                  per_head(c, c), per_head(c, L), per_head(c, L), per_head(1, L),
                  cols(), cols(), cols(), gain()],
        out_specs=[cols(), cols()],
        out_shape=[jax.ShapeDtypeStruct((b, s, RET_WIDTH), BF16),
                   jax.ShapeDtypeStruct((b, s, ATT_WIDTH), BF16)],
        scratch_shapes=[pltpu.VMEM((gps, HEADS_PER_LANE_TILE, nb, ATT_HEAD_DIM + ONES_ROWS, MOBA_BLOCK),
                                   BF16)],
        compiler_params=_params(),
        name="mixers",
    )(rq, rk, rv, rg, ret_beta, tables["inner_decay"], tables["q_decay"], tables["k_decay"],
      tables["chunk_decay"], aq, ak, av, att_beta)


def _outproj_ffn_kernel(x1_ref, ret_ref, att_ref, wo_ret_ref, wo_att_ref, n_ref, wg_ref, wu_ref, wd_ref,
                        o_ref):
    half = x1_ref.shape[0] // 2
    halves = (slice(0, half), slice(half, 2 * half))
    x2 = [x1_ref[r, :]
          + jnp.dot(ret_ref[r, :], wo_ret_ref[...], preferred_element_type=F32)
          + jnp.dot(att_ref[r, :], wo_att_ref[...], preferred_element_type=F32) for r in halves]
    h = [(_rms_rows(t) * n_ref[...]).astype(BF16) for t in x2]
    a = [_swiglu_activation(t, wg_ref, wu_ref) for t in h]
    for r, t, act in zip(halves, x2, a):
        o_ref[r, :] = t + 0.5 * jnp.dot(act, wd_ref[...], preferred_element_type=F32)


def _outproj_ffn(x1, ret, att, seq, p):
    tokens = x1.shape[0]
    tm = _tiles(seq)["row_tile"]
    row = lambda w: pl.BlockSpec((tm, w), lambda i: (i, 0))
    return pl.pallas_call(
        _outproj_ffn_kernel,
        grid=(tokens // tm,),
        in_specs=[row(D_MODEL), row(RET_WIDTH), row(ATT_WIDTH),
                  _resident((RET_WIDTH, D_MODEL)), _resident((ATT_WIDTH, D_MODEL)),
                  _resident((1, D_MODEL)),
                  _resident((D_MODEL, FFN_DIM)), _resident((D_MODEL, FFN_DIM)),
                  _resident((FFN_DIM, D_MODEL))],
        out_specs=row(D_MODEL),
        out_shape=jax.ShapeDtypeStruct((tokens, D_MODEL), F32),
        compiler_params=_params(),
        name="outproj_ffn2",
    )(x1, ret, att, p["wo_ret"], p["wo_att"], p["n3"], p["wg2"], p["wu2"], p["wd2"])


def _tables(seq):
    pos = jnp.arange(seq, dtype=F32)[:, None]

    def angles(theta, rot_dim):
        half = rot_dim // 2
        inv_freq = theta ** (-jnp.arange(half, dtype=F32) * 2.0 / rot_dim)
        return pos * inv_freq[None, :]

    ang = angles(RET_ROPE_BASE, RET_HEAD_DIM)
    rcos = jnp.concatenate([jnp.cos(ang), jnp.cos(ang)], axis=1)
    rsin = jnp.concatenate([-jnp.sin(ang), jnp.sin(ang)], axis=1)

    ang = angles(ATT_ROPE_BASE, ATT_ROT_DIM)
    half = ATT_ROT_DIM // 2
    pad = jnp.zeros((seq, ATT_HEAD_DIM - ATT_ROT_DIM), F32)
    zero_half = jnp.zeros((seq, half), F32)
    head_cos = jnp.concatenate([jnp.cos(ang), jnp.cos(ang), 1.0 + pad], axis=1)
    head_sin_lo = jnp.concatenate([-jnp.sin(ang), zero_half, pad], axis=1)
    head_sin_hi = jnp.concatenate([zero_half, jnp.sin(ang), pad], axis=1)
    rep = lambda t: jnp.tile(t, (1, HEADS_PER_LANE_TILE))

    c = RET_KERNEL_CHUNK
    log_g = jnp.log(1.0 - 2.0 ** (-5.0 - jnp.arange(RET_HEADS, dtype=F32)))
    i = jnp.arange(c, dtype=F32)
    diff = i[:, None] - i[None, :]
    inner = jnp.where(diff[None] >= 0,
                      jnp.exp(jnp.maximum(diff, 0.0)[None] * log_g[:, None, None]), 0.0)
    lanes = lambda t: jnp.broadcast_to(t[:, :, None], (RET_HEADS, t.shape[1], RET_HEAD_DIM))
    q_decay = lanes(jnp.exp((i[None, :] + 1.0) * log_g[:, None]))
    k_decay = lanes(jnp.exp((c - 1.0 - i[None, :]) * log_g[:, None]))
    chunk_decay = lanes(jnp.exp(c * log_g)[:, None])

    return dict(rcos=rcos, rsin=rsin, acos=rep(head_cos), asin_lo=rep(head_sin_lo),
                asin_hi=rep(head_sin_hi), inner_decay=inner, q_decay=q_decay, k_decay=k_decay,
                chunk_decay=chunk_decay)


def kernel(x, ffn1_norm_w, ffn1_w_gate, ffn1_w_up, ffn1_w_down, mix_norm_w, w_in, ret_out_beta,
           q_norm_w, k_norm_w, att_out_beta, w_out, ffn2_norm_w, ffn2_w_gate, ffn2_w_up, ffn2_w_down):
    b, s, d = x.shape
    assert d == D_MODEL
    tables = _tables(s)
    depth = ffn1_norm_w.shape[0]
    q_logit_scale = ATT_HEAD_DIM ** -0.5 * math.log2(math.e)
    x2d = x.reshape(b * s, d)
    for l in range(depth):
        p = dict(
            n1=ffn1_norm_w[l][None, :], wg1=ffn1_w_gate[l].astype(BF16), wu1=ffn1_w_up[l].astype(BF16),
            wd1=ffn1_w_down[l].astype(BF16), n2=mix_norm_w[l][None, :], w_in=w_in[l].astype(BF16),
            qg=jnp.tile(q_norm_w[l] * q_logit_scale, ATT_HEADS)[None, :],
            kg=jnp.tile(k_norm_w[l], ATT_HEADS)[None, :],
            wo_ret=w_out[l][:RET_WIDTH].astype(BF16), wo_att=w_out[l][RET_WIDTH:].astype(BF16),
            n3=ffn2_norm_w[l][None, :], wg2=ffn2_w_gate[l].astype(BF16), wu2=ffn2_w_up[l].astype(BF16),
            wd2=ffn2_w_down[l].astype(BF16))
        x1, rq, rk, rv, rg, aq, ak, av = _ffn_inproj(x2d, s, p, tables)
        seq3 = lambda t: t.reshape(b, s, t.shape[-1])
        ret, att = _mixers(seq3(rq), seq3(rk), seq3(rv), seq3(rg), ret_out_beta[l][None, :],
                           seq3(aq), seq3(ak), seq3(av), att_out_beta[l][None, :], tables)
        x2d = _outproj_ffn(x1, ret.reshape(b * s, RET_WIDTH), att.reshape(b * s, ATT_WIDTH), s, p)
    return x2d.reshape(b, s, d)
```

```python
import math

import numpy as np
import jax
import jax.numpy as jnp
from jax import lax
from jax.experimental import pallas as pl
from jax.experimental.pallas import tpu as pltpu

D_MODEL = 1024
RET_HEADS = 4
RET_HEAD_DIM = 128
RET_WIDTH = RET_HEADS * RET_HEAD_DIM
RET_ROPE_BASE = 10000.0
ATT_HEADS = 8
ATT_HEAD_DIM = 64
ATT_WIDTH = ATT_HEADS * ATT_HEAD_DIM
ATT_ROT_DIM = ATT_HEAD_DIM // 4
ATT_ROPE_BASE = 500000.0
MOBA_BLOCK = 256
MOBA_TOPK = 3
FFN_DIM = 2816
NORM_EPS = 1e-6

V7X_LANES = 128
V7X_MXU_DIM = 256
V7X_VMEM_LIMIT_BYTES = 56 * 1024 * 1024

F32 = jnp.float32
BF16 = jnp.bfloat16
HEADS_PER_LANE_TILE = V7X_LANES // ATT_HEAD_DIM
ATT_LANE_TILES = ATT_WIDTH // V7X_LANES
RET_KERNEL_CHUNK = V7X_MXU_DIM

NT = (((1,), (1,)), ((), ()))
TN = (((0,), (0,)), ((), ()))


def _tiles(seq):
    row_tile = 512 if seq % 512 == 0 else 256
    assert seq % row_tile == 0 and seq % MOBA_BLOCK == 0 and seq % RET_KERNEL_CHUNK == 0
    return dict(row_tile=row_tile)


def _params(dimension_semantics=None):
    return pltpu.CompilerParams(vmem_limit_bytes=V7X_VMEM_LIMIT_BYTES,
                                dimension_semantics=dimension_semantics)


def _resident(shape):
    zeros = (0,) * len(shape)
    return pl.BlockSpec(shape, lambda *_: zeros, pipeline_mode=pl.Buffered(1))


def _rms_rows(x):
    return x * lax.rsqrt(jnp.mean(x * x, axis=-1, keepdims=True) + NORM_EPS)


def _swiglu_activation(h, wg_ref, wu_ref):
    g = jnp.dot(h, wg_ref[...], preferred_element_type=F32)
    u = jnp.dot(h, wu_ref[...], preferred_element_type=F32)
    return (g * jax.nn.sigmoid(g) * u).astype(BF16)


def _swiglu_half_step(x, norm_w, wg_ref, wu_ref, wd_ref):
    h = (_rms_rows(x) * norm_w).astype(BF16)
    a = _swiglu_activation(h, wg_ref, wu_ref)
    return x + 0.5 * jnp.dot(a, wd_ref[...], preferred_element_type=F32)


def _split_bf16(x):
    hi = x.astype(BF16)
    lo = (x - hi.astype(F32)).astype(BF16)
    return hi, lo


def _ffn_inproj_kernel(x_ref, n1_ref, wg_ref, wu_ref, wd_ref, n2_ref, win_ref,
                       rcos_ref, rsin_ref, acos_ref, asin_lo_ref, asin_hi_ref,
                       qg_ref, kg_ref,
                       x1_ref, rq_ref, rk_ref, rv_ref, rg_ref, aq_ref, ak_ref, av_ref):
    R, A, L = RET_WIDTH, ATT_WIDTH, V7X_LANES
    half = x_ref.shape[0] // 2
    halves = (slice(0, half), slice(half, 2 * half))

    h = [(_rms_rows(x_ref[r, :]) * n1_ref[...]).astype(BF16) for r in halves]
    act = [_swiglu_activation(t, wg_ref, wu_ref) for t in h]
    x1 = [x_ref[r, :] + 0.5 * jnp.dot(a, wd_ref[...], preferred_element_type=F32)
          for r, a in zip(halves, act)]
    for r, t in zip(halves, x1):
        x1_ref[r, :] = t
    hn = [(_rms_rows(t) * n2_ref[...]).astype(BF16) for t in x1]

    half_rot = ATT_ROT_DIM // 2
    first_head = lax.broadcasted_iota(jnp.int32, (half, L), 1) < ATT_HEAD_DIM

    def project(r, hn_r):
        def proj(col0, width):
            return jnp.dot(hn_r, win_ref[:, col0:col0 + width], preferred_element_type=F32)

        def ret_rope(p, out_ref, scale):
            for hd in range(RET_HEADS):
                t = p[:, hd * L:(hd + 1) * L]
                o = t * rcos_ref[r, :] + pltpu.roll(t, RET_HEAD_DIM // 2, 1) * rsin_ref[r, :]
                if scale is not None:
                    o = o * scale
                out_ref[r, hd * L:(hd + 1) * L] = o.astype(out_ref.dtype)

        def att_norm_rope(p, gain_ref, out_ref):
            for c in range(A // L):
                t = p[:, c * L:(c + 1) * L]
                sq = t * t
                ss = jnp.where(first_head,
                               jnp.sum(jnp.where(first_head, sq, 0.0), axis=-1, keepdims=True),
                               jnp.sum(jnp.where(first_head, 0.0, sq), axis=-1, keepdims=True))
                u = (t * lax.rsqrt(ss * (1.0 / ATT_HEAD_DIM) + NORM_EPS)
                     * gain_ref[:, c * L:(c + 1) * L])
                o = (u * acos_ref[r, :] + pltpu.roll(u, half_rot, 1) * asin_hi_ref[r, :]
                     + pltpu.roll(u, L - half_rot, 1) * asin_lo_ref[r, :])
                out_ref[r, c * L:(c + 1) * L] = o.astype(out_ref.dtype)

        o0 = 4 * R
        att_norm_rope(proj(o0, A), qg_ref, aq_ref)
        att_norm_rope(proj(o0 + A, A), kg_ref, ak_ref)
        ret_rope(proj(0, R), rq_ref, None)
        ret_rope(proj(R, R), rk_ref, RET_HEAD_DIM ** -0.5)
        rv_ref[r, :] = proj(2 * R, R).astype(rv_ref.dtype)
        rg_ref[r, :] = proj(3 * R, R).astype(rg_ref.dtype)
        av_ref[r, :] = proj(o0 + 2 * A, A).astype(av_ref.dtype)

    for r, t in zip(halves, hn):
        project(r, t)


def _ffn_inproj(x2d, seq, p, tables):
    tokens = x2d.shape[0]
    tm = _tiles(seq)["row_tile"]
    pos_blocks = seq // tm
    row = lambda w: pl.BlockSpec((tm, w), lambda i: (i, 0))
    pos = lambda: pl.BlockSpec((tm, V7X_LANES), lambda i: (i % pos_blocks, 0))
    in_width = p["w_in"].shape[1]
    out_shapes = ([jax.ShapeDtypeStruct((tokens, D_MODEL), F32)]
                  + [jax.ShapeDtypeStruct((tokens, RET_WIDTH), BF16)] * 3
                  + [jax.ShapeDtypeStruct((tokens, RET_WIDTH), F32)]
                  + [jax.ShapeDtypeStruct((tokens, ATT_WIDTH), BF16)] * 3)
    return pl.pallas_call(
        _ffn_inproj_kernel,
        grid=(tokens // tm,),
        in_specs=[row(D_MODEL), _resident((1, D_MODEL)),
                  _resident((D_MODEL, FFN_DIM)), _resident((D_MODEL, FFN_DIM)),
                  _resident((FFN_DIM, D_MODEL)), _resident((1, D_MODEL)),
                  _resident((D_MODEL, in_width)),
                  pos(), pos(), pos(), pos(), pos(),
                  _resident((1, ATT_WIDTH)), _resident((1, ATT_WIDTH))],
        out_specs=[row(D_MODEL)] + [row(RET_WIDTH)] * 4 + [row(ATT_WIDTH)] * 3,
        out_shape=out_shapes,
        compiler_params=_params(),
        name="ffn1_inproj",
    )(x2d, p["n1"], p["wg1"], p["wu1"], p["wd1"], p["n2"], p["w_in"],
      tables["rcos"], tables["rsin"], tables["acos"], tables["asin_lo"], tables["asin_hi"],
      p["qg"], p["kg"])


def _retention_stream(q_ref, k_ref, v_ref, g_ref, beta_ref, inner_ref, qdec_ref, kdec_ref, cdec_ref,
                      o_ref):
    c = RET_KERNEL_CHUNK
    n_chunks = q_ref.shape[1] // c
    rows = lambda i: slice(i * c, (i + 1) * c)
    live = {"state": None}

    def begin(i):
        live["scores", i] = lax.dot_general(q_ref[0, rows(i), :], k_ref[0, rows(i), :], NT,
                                            preferred_element_type=F32) * inner_ref[0]
        if i < n_chunks - 1:
            kd = (k_ref[0, rows(i), :].astype(F32) * kdec_ref[0]).astype(BF16)
            live["increment", i] = lax.dot_general(kd, v_ref[0, rows(i), :], TN,
                                                   preferred_element_type=F32)

    def finish(i):
        o = jnp.dot(live.pop(("scores", i)).astype(BF16), v_ref[0, rows(i), :],
                    preferred_element_type=F32)
        state = live["state"]
        if state is not None:
            o = o + jnp.dot(q_ref[0, rows(i), :], state.astype(BF16),
                            preferred_element_type=F32) * qdec_ref[0]
        if i < n_chunks - 1:
            inc = live.pop(("increment", i))
            live["state"] = inc if state is None else state * cdec_ref[0] + inc
        g = g_ref[0, rows(i), :]
        o_ref[0, rows(i), :] = (_rms_rows(o) * (g * jax.nn.sigmoid(g)) * beta_ref[...]).astype(o_ref.dtype)

    return begin, finish


ONES_ROWS = 16
STAT_ROWS = 8
KMEAN_ROWS = 16
SCORE_LOOKAHEAD = 6
GROUPS_PER_STEP = 2
RET_CHAIN_STRIDE = 8
RET_FINISH_AFTER = 3


def _mixer_kernel(rq_ref, rk_ref, rv_ref, rg_ref, rbeta_ref, inner_ref, qdec_ref, kdec_ref, cdec_ref,
                  q_ref, k_ref, v_ref, beta_ref, ret_ref, o_ref, vt_ref):
    bs = MOBA_BLOCK
    seq = k_ref.shape[1]
    nb = seq // bs
    dh = ATT_HEAD_DIM
    L = V7X_LANES
    heads = range(HEADS_PER_LANE_TILE)
    groups = range(GROUPS_PER_STEP)
    lanes = lambda g: slice(g * L, (g + 1) * L)

    ret_hooks = [_retention_stream(rq_ref.at[:, :, lanes(g)], rk_ref.at[:, :, lanes(g)],
                                   rv_ref.at[:, :, lanes(g)], rg_ref.at[:, :, lanes(g)],
                                   rbeta_ref.at[:, lanes(g)], inner_ref.at[g:g + 1],
                                   qdec_ref.at[g:g + 1], kdec_ref.at[g:g + 1], cdec_ref.at[g:g + 1],
                                   ret_ref.at[:, :, lanes(g)]) for g in groups]

    km_hi, km_lo = {}, {}
    for g in groups:
        vt = jnp.transpose(v_ref[0, :, lanes(g)].astype(F32))
        for h in heads:
            for n in range(nb):
                vt_ref[g, h, n, :dh, :] = vt[h * dh:(h + 1) * dh, n * bs:(n + 1) * bs].astype(BF16)
                vt_ref[g, h, n, dh:, :] = jnp.ones((ONES_ROWS, bs), BF16)
        kmean = jnp.mean(k_ref[0, :, lanes(g)].astype(F32).reshape(nb, bs, L), axis=1)
        kmean = jnp.concatenate([kmean, jnp.zeros((KMEAN_ROWS - nb, L), F32)], axis=0)
        km_hi[g], km_lo[g] = _split_bf16(kmean)

    lane = lax.broadcasted_iota(jnp.int32, (bs, L), 1)
    head_lanes = [(lane >= h * dh) & (lane < (h + 1) * dh) for h in heads]
    row_id = lax.broadcasted_iota(jnp.int32, (KMEAN_ROWS, bs), 0)
    causal_t = (lax.broadcasted_iota(jnp.int32, (bs, bs), 0)
                <= lax.broadcasted_iota(jnp.int32, (bs, bs), 1))
    stat = lambda t: jnp.broadcast_to(t, (STAT_ROWS, bs))
    wide = lambda t: jnp.broadcast_to(t[:1], (dh, bs))

    masked_q = {}

    def query(g, j, h):
        if (g, j, h) not in masked_q:
            q = q_ref[0, j * bs:(j + 1) * bs, lanes(g)]
            masked_q[g, j, h] = jnp.where(head_lanes[h], q, jnp.zeros_like(q))
        return masked_q[g, j, h]

    selected = {}

    def selection(g, j, h):
        if (g, j, h) not in selected:
            sel = [None] * j
            if j > MOBA_TOPK:
                gate = (lax.dot_general(km_hi[g], query(g, j, h), NT, preferred_element_type=F32)
                        + lax.dot_general(km_lo[g], query(g, j, h), NT, preferred_element_type=F32))
                gate = jnp.where(row_id < j, gate, -jnp.inf)
                rank = jnp.zeros(gate.shape, jnp.int32)
                for m in range(j):
                    other = gate[m:m + 1, :]
                    ahead = (other > gate) | ((other == gate) & (m < row_id))
                    rank = rank + ahead.astype(jnp.int32)
                keep = (rank < MOBA_TOPK).astype(F32)
                sel = [stat(keep[n:n + 1, :]) > 0.5 for n in range(j)]
            selected[g, j, h] = sel
        return selected[g, j, h]

    def combine(g, j, h, pv):
        sel = selection(g, j, h)
        m_all = pv[j][0]
        for n in range(j):
            m_n = pv[n][0]
            m_all = jnp.maximum(m_all, m_n if sel[n] is None else jnp.where(sel[n], m_n, -jnp.inf))
        l_sum = jnp.zeros((STAT_ROWS, bs), F32)
        acc = jnp.zeros((dh, bs), F32)
        for n in range(j + 1):
            m_n, ov = pv[n]
            w = jnp.exp2(m_n - m_all)
            if n < j and sel[n] is not None:
                w = jnp.where(sel[n], w, 0.0)
            l_sum = l_sum + w * ov[dh:dh + STAT_ROWS, :]
            acc = acc + wide(w) * ov[:dh, :]
        res = acc * wide(1.0 / l_sum)
        ms = jnp.mean(res * res, axis=0, keepdims=True)
        return res * lax.rsqrt(ms + NORM_EPS)

    group_work = [(j, n, h) for j in reversed(range(nb)) for n in range(j + 1) for h in heads]
    work = [(g,) + w for g in groups for w in group_work]
    assert RET_CHAIN_STRIDE * (nb - 1) + RET_FINISH_AFTER < len(group_work)

    def retention_hook(kind, chain):
        g, since = divmod(chain, len(group_work))
        since -= kind * RET_FINISH_AFTER
        if since >= 0 and since % RET_CHAIN_STRIDE == 0 and since // RET_CHAIN_STRIDE < nb:
            ret_hooks[g][kind](since // RET_CHAIN_STRIDE)

    scores, pv, normed = {}, {}, {}
    for step in range(-SCORE_LOOKAHEAD, len(work)):
        if step + SCORE_LOOKAHEAD < len(work):
            g, j, n, h = work[step + SCORE_LOOKAHEAD]
            scores[g, j, n, h] = lax.dot_general(k_ref[0, n * bs:(n + 1) * bs, lanes(g)],
                                                 query(g, j, h), NT,
                                                 preferred_element_type=F32)
        if step < 0:
            continue
        g, j, n, h = work[step]
        retention_hook(0, step)
        if n == 0:
            selection(g, j, h)
        st = scores.pop((g, j, n, h))
        if n == j:
            st = jnp.where(causal_t, st, -jnp.inf)
        mt = jnp.max(st, axis=0, keepdims=True)
        pt = jnp.exp2(st - mt).astype(BF16)
        pv.setdefault((g, j, h), {})[n] = (stat(mt), jnp.dot(vt_ref[g, h, n], pt,
                                                             preferred_element_type=F32))
        retention_hook(1, step)
        if n == j:
            normed[g, j, h] = combine(g, j, h, pv.pop((g, j, h)))
            if h == heads[-1]:
                att = jnp.transpose(jnp.concatenate([normed.pop((g, j, hh)) for hh in heads], axis=0))
                o_ref[0, j * bs:(j + 1) * bs, lanes(g)] = (
                    att * beta_ref[:, lanes(g)]).astype(o_ref.dtype)


def _mixers(rq, rk, rv, rg, ret_beta, aq, ak, av, att_beta, tables):
    b, s, _ = aq.shape
    L = V7X_LANES
    nb = s // MOBA_BLOCK
    c = RET_KERNEL_CHUNK
    assert nb <= KMEAN_ROWS
    assert RET_HEADS == ATT_LANE_TILES and RET_HEAD_DIM == L and c == MOBA_BLOCK
    gps = GROUPS_PER_STEP
    assert ATT_LANE_TILES % gps == 0
    cols = lambda: pl.BlockSpec((1, s, gps * L), lambda bi, t: (bi, 0, t))
    gain = lambda: pl.BlockSpec((1, gps * L), lambda bi, t: (0, t))
    per_head = lambda r, w: pl.BlockSpec((gps, r, w), lambda bi, t: (t, 0, 0))
    return pl.pallas_call(
        _mixer_kernel,
        grid=(b, ATT_LANE_TILES // gps),
        in_specs=[cols(), cols(), cols(), cols(), gain(),
                  per_head(c, c), per_head(c, L), per_head(c, L), per_head(1, L),
                  cols(), cols(), cols(), gain()],
        out_specs=[cols(), cols()],
        out_shape=[jax.ShapeDtypeStruct((b, s, RET_WIDTH), BF16),
                   jax.ShapeDtypeStruct((b, s, ATT_WIDTH), BF16)],
        scratch_shapes=[pltpu.VMEM((gps, HEADS_PER_LANE_TILE, nb, ATT_HEAD_DIM + ONES_ROWS, MOBA_BLOCK),
                                   BF16)],
        compiler_params=_params(),
        name="mixers",
    )(rq, rk, rv, rg, ret_beta, tables["inner_decay"], tables["q_decay"], tables["k_decay"],
      tables["chunk_decay"], aq, ak, av, att_beta)


def _outproj_ffn_kernel(x1_ref, ret_ref, att_ref, wo_ret_ref, wo_att_ref, n_ref, wg_ref, wu_ref, wd_ref,
                        o_ref):
    half = x1_ref.shape[0] // 2
    halves = (slice(0, half), slice(half, 2 * half))
    x2 = [x1_ref[r, :]
          + jnp.dot(ret_ref[r, :], wo_ret_ref[...], preferred_element_type=F32)
          + jnp.dot(att_ref[r, :], wo_att_ref[...], preferred_element_type=F32) for r in halves]
    h = [(_rms_rows(t) * n_ref[...]).astype(BF16) for t in x2]
    a = [_swiglu_activation(t, wg_ref, wu_ref) for t in h]
    for r, t, act in zip(halves, x2, a):
        o_ref[r, :] = t + 0.5 * jnp.dot(act, wd_ref[...], preferred_element_type=F32)


def _outproj_ffn(x1, ret, att, seq, p):
    tokens = x1.shape[0]
    tm = _tiles(seq)["row_tile"]
    row = lambda w: pl.BlockSpec((tm, w), lambda i: (i, 0))
    return pl.pallas_call(
        _outproj_ffn_kernel,
        grid=(tokens // tm,),
        in_specs=[row(D_MODEL), row(RET_WIDTH), row(ATT_WIDTH),
                  _resident((RET_WIDTH, D_MODEL)), _resident((ATT_WIDTH, D_MODEL)),
                  _resident((1, D_MODEL)),
                  _resident((D_MODEL, FFN_DIM)), _resident((D_MODEL, FFN_DIM)),
                  _resident((FFN_DIM, D_MODEL))],
        out_specs=row(D_MODEL),
        out_shape=jax.ShapeDtypeStruct((tokens, D_MODEL), F32),
        compiler_params=_params(),
        name="outproj_ffn2",
    )(x1, ret, att, p["wo_ret"], p["wo_att"], p["n3"], p["wg2"], p["wu2"], p["wd2"])


def _tables(seq):
    pos = np.arange(seq, dtype=np.float64)[:, None]

    def angles(theta, rot_dim):
        half = rot_dim // 2
        inv_freq = theta ** (-np.arange(half, dtype=np.float64) * 2.0 / rot_dim)
        return pos * inv_freq[None, :]

    ang = angles(RET_ROPE_BASE, RET_HEAD_DIM)
    rcos = np.concatenate([np.cos(ang), np.cos(ang)], axis=1)
    rsin = np.concatenate([-np.sin(ang), np.sin(ang)], axis=1)

    ang = angles(ATT_ROPE_BASE, ATT_ROT_DIM)
    half = ATT_ROT_DIM // 2
    pad = np.zeros((seq, ATT_HEAD_DIM - ATT_ROT_DIM))
    zero_half = np.zeros((seq, half))
    head_cos = np.concatenate([np.cos(ang), np.cos(ang), 1.0 + pad], axis=1)
    head_sin_lo = np.concatenate([-np.sin(ang), zero_half, pad], axis=1)
    head_sin_hi = np.concatenate([zero_half, np.sin(ang), pad], axis=1)
    rep = lambda t: np.tile(t, (1, HEADS_PER_LANE_TILE))

    c = RET_KERNEL_CHUNK
    log_g = np.log(1.0 - 2.0 ** (-5.0 - np.arange(RET_HEADS, dtype=np.float64)))
    i = np.arange(c, dtype=np.float64)
    diff = i[:, None] - i[None, :]
    inner = np.where(diff[None] >= 0,
                     np.exp(np.maximum(diff, 0.0)[None] * log_g[:, None, None]), 0.0)
    lanes = lambda t: np.broadcast_to(t[:, :, None], (RET_HEADS, t.shape[1], RET_HEAD_DIM))
    q_decay = lanes(np.exp((i[None, :] + 1.0) * log_g[:, None]))
    k_decay = lanes(np.exp((c - 1.0 - i[None, :]) * log_g[:, None]))
    chunk_decay = lanes(np.exp(c * log_g)[:, None])

    tables = dict(rcos=rcos, rsin=rsin, acos=rep(head_cos), asin_lo=rep(head_sin_lo),
                  asin_hi=rep(head_sin_hi), inner_decay=inner, q_decay=q_decay, k_decay=k_decay,
                  chunk_decay=chunk_decay)
    return {name: jnp.asarray(np.ascontiguousarray(t), dtype=F32) for name, t in tables.items()}


def kernel(x, ffn1_norm_w, ffn1_w_gate, ffn1_w_up, ffn1_w_down, mix_norm_w, w_in, ret_out_beta,
           q_norm_w, k_norm_w, att_out_beta, w_out, ffn2_norm_w, ffn2_w_gate, ffn2_w_up, ffn2_w_down):
    b, s, d = x.shape
    assert d == D_MODEL
    tables = _tables(s)
    depth = ffn1_norm_w.shape[0]
    q_logit_scale = ATT_HEAD_DIM ** -0.5 * math.log2(math.e)
    x2d = x.reshape(b * s, d)
    for l in range(depth):
        p = dict(
            n1=ffn1_norm_w[l][None, :], wg1=ffn1_w_gate[l].astype(BF16), wu1=ffn1_w_up[l].astype(BF16),
            wd1=ffn1_w_down[l].astype(BF16), n2=mix_norm_w[l][None, :], w_in=w_in[l].astype(BF16),
            qg=jnp.tile(q_norm_w[l] * q_logit_scale, ATT_HEADS)[None, :],
            kg=jnp.tile(k_norm_w[l], ATT_HEADS)[None, :],
            wo_ret=w_out[l][:RET_WIDTH].astype(BF16), wo_att=w_out[l][RET_WIDTH:].astype(BF16),
            n3=ffn2_norm_w[l][None, :], wg2=ffn2_w_gate[l].astype(BF16), wu2=ffn2_w_up[l].astype(BF16),
            wd2=ffn2_w_down[l].astype(BF16))
        x1, rq, rk, rv, rg, aq, ak, av = _ffn_inproj(x2d, s, p, tables)
        seq3 = lambda t: t.reshape(b, s, t.shape[-1])
        ret, att = _mixers(seq3(rq), seq3(rk), seq3(rv), seq3(rg), ret_out_beta[l][None, :],
                           seq3(aq), seq3(ak), seq3(av), att_out_beta[l][None, :], tables)
        x2d = _outproj_ffn(x1, ret.reshape(b * s, RET_WIDTH), att.reshape(b * s, ATT_WIDTH), s, p)
    return x2d.reshape(b, s, d)
```

```python
import math

import numpy as np
import jax
import jax.numpy as jnp
from jax import lax
from jax.experimental import pallas as pl
from jax.experimental.pallas import tpu as pltpu

D_MODEL = 1024
RET_HEADS = 4
RET_HEAD_DIM = 128
RET_WIDTH = RET_HEADS * RET_HEAD_DIM
RET_ROPE_BASE = 10000.0
ATT_HEADS = 8
ATT_HEAD_DIM = 64
ATT_WIDTH = ATT_HEADS * ATT_HEAD_DIM
ATT_ROT_DIM = ATT_HEAD_DIM // 4
ATT_ROPE_BASE = 500000.0
MOBA_BLOCK = 256
MOBA_TOPK = 3
FFN_DIM = 2816
NORM_EPS = 1e-6

V7X_LANES = 128
V7X_MXU_DIM = 256
V7X_VMEM_LIMIT_BYTES = 56 * 1024 * 1024

F32 = jnp.float32
BF16 = jnp.bfloat16
HEADS_PER_LANE_TILE = V7X_LANES // ATT_HEAD_DIM
ATT_LANE_TILES = ATT_WIDTH // V7X_LANES
RET_KERNEL_CHUNK = V7X_MXU_DIM

NT = (((1,), (1,)), ((), ()))
TN = (((0,), (0,)), ((), ()))


def _tiles(seq):
    row_tile = 512 if seq % 512 == 0 else 256
    assert seq % row_tile == 0 and seq % MOBA_BLOCK == 0 and seq % RET_KERNEL_CHUNK == 0
    return dict(row_tile=row_tile)


def _params(dimension_semantics=None):
    return pltpu.CompilerParams(vmem_limit_bytes=V7X_VMEM_LIMIT_BYTES,
                                dimension_semantics=dimension_semantics)


def _resident(shape):
    zeros = (0,) * len(shape)
    return pl.BlockSpec(shape, lambda *_: zeros, pipeline_mode=pl.Buffered(1))


def _rms_rows(x):
    return x * lax.rsqrt(jnp.mean(x * x, axis=-1, keepdims=True) + NORM_EPS)


def _swiglu_activation(h, wg_ref, wu_ref):
    g = jnp.dot(h, wg_ref[...], preferred_element_type=F32)
    u = jnp.dot(h, wu_ref[...], preferred_element_type=F32)
    return (g * jax.nn.sigmoid(g) * u).astype(BF16)


def _swiglu_half_step(x, norm_w, wg_ref, wu_ref, wd_ref):
    h = (_rms_rows(x) * norm_w).astype(BF16)
    a = _swiglu_activation(h, wg_ref, wu_ref)
    return x + 0.5 * jnp.dot(a, wd_ref[...], preferred_element_type=F32)


def _split_bf16(x):
    hi = x.astype(BF16)
    lo = (x - hi.astype(F32)).astype(BF16)
    return hi, lo


def _ffn_inproj_kernel(x_ref, n1_ref, wg_ref, wu_ref, wd_ref, n2_ref, win_ref,
                       rcos_ref, rsin_ref, acos_ref, asin_lo_ref, asin_hi_ref,
                       qg_ref, kg_ref,
                       x1_ref, rq_ref, rk_ref, rv_ref, rg_ref, aq_ref, ak_ref, av_ref):
    R, A, L = RET_WIDTH, ATT_WIDTH, V7X_LANES
    half = x_ref.shape[0] // 2
    halves = (slice(0, half), slice(half, 2 * half))

    h = [(_rms_rows(x_ref[r, :]) * n1_ref[...]).astype(BF16) for r in halves]
    act = [_swiglu_activation(t, wg_ref, wu_ref) for t in h]
    x1 = [x_ref[r, :] + 0.5 * jnp.dot(a, wd_ref[...], preferred_element_type=F32)
          for r, a in zip(halves, act)]
    for r, t in zip(halves, x1):
        x1_ref[r, :] = t
    hn = [(_rms_rows(t) * n2_ref[...]).astype(BF16) for t in x1]

    half_rot = ATT_ROT_DIM // 2
    first_head = lax.broadcasted_iota(jnp.int32, (half, L), 1) < ATT_HEAD_DIM

    def project(r, hn_r):
        def proj(col0, width):
            return jnp.dot(hn_r, win_ref[:, col0:col0 + width], preferred_element_type=F32)

        def ret_rope(p, out_ref, scale):
            for hd in range(RET_HEADS):
                t = p[:, hd * L:(hd + 1) * L]
                o = t * rcos_ref[r, :] + pltpu.roll(t, RET_HEAD_DIM // 2, 1) * rsin_ref[r, :]
                if scale is not None:
                    o = o * scale
                out_ref[r, hd * L:(hd + 1) * L] = o.astype(out_ref.dtype)

        def att_norm_rope(p, gain_ref, out_ref):
            for c in range(A // L):
                t = p[:, c * L:(c + 1) * L]
                sq = t * t
                ss = jnp.where(first_head,
                               jnp.sum(jnp.where(first_head, sq, 0.0), axis=-1, keepdims=True),
                               jnp.sum(jnp.where(first_head, 0.0, sq), axis=-1, keepdims=True))
                u = (t * lax.rsqrt(ss * (1.0 / ATT_HEAD_DIM) + NORM_EPS)
                     * gain_ref[:, c * L:(c + 1) * L])
                o = (u * acos_ref[r, :] + pltpu.roll(u, half_rot, 1) * asin_hi_ref[r, :]
                     + pltpu.roll(u, L - half_rot, 1) * asin_lo_ref[r, :])
                out_ref[r, c * L:(c + 1) * L] = o.astype(out_ref.dtype)

        o0 = 4 * R
        att_norm_rope(proj(o0, A), qg_ref, aq_ref)
        att_norm_rope(proj(o0 + A, A), kg_ref, ak_ref)
        ret_rope(proj(0, R), rq_ref, None)
        ret_rope(proj(R, R), rk_ref, RET_HEAD_DIM ** -0.5)
        rv_ref[r, :] = proj(2 * R, R).astype(rv_ref.dtype)
        rg_ref[r, :] = proj(3 * R, R).astype(rg_ref.dtype)
        av_ref[r, :] = proj(o0 + 2 * A, A).astype(av_ref.dtype)

    for r, t in zip(halves, hn):
        project(r, t)


def _ffn_inproj(x2d, seq, p, tables):
    tokens = x2d.shape[0]
    tm = _tiles(seq)["row_tile"]
    pos_blocks = seq // tm
    row = lambda w: pl.BlockSpec((tm, w), lambda i: (i, 0))
    pos = lambda: pl.BlockSpec((tm, V7X_LANES), lambda i: (i % pos_blocks, 0))
    in_width = p["w_in"].shape[1]
    out_shapes = ([jax.ShapeDtypeStruct((tokens, D_MODEL), F32)]
                  + [jax.ShapeDtypeStruct((tokens, RET_WIDTH), BF16)] * 3
                  + [jax.ShapeDtypeStruct((tokens, RET_WIDTH), F32)]
                  + [jax.ShapeDtypeStruct((tokens, ATT_WIDTH), BF16)] * 3)
    return pl.pallas_call(
        _ffn_inproj_kernel,
        grid=(tokens // tm,),
        in_specs=[row(D_MODEL), _resident((1, D_MODEL)),
                  _resident((D_MODEL, FFN_DIM)), _resident((D_MODEL, FFN_DIM)),
                  _resident((FFN_DIM, D_MODEL)), _resident((1, D_MODEL)),
                  _resident((D_MODEL, in_width)),
                  pos(), pos(), pos(), pos(), pos(),
                  _resident((1, ATT_WIDTH)), _resident((1, ATT_WIDTH))],
        out_specs=[row(D_MODEL)] + [row(RET_WIDTH)] * 4 + [row(ATT_WIDTH)] * 3,
        out_shape=out_shapes,
        compiler_params=_params(),
        name="ffn1_inproj",
    )(x2d, p["n1"], p["wg1"], p["wu1"], p["wd1"], p["n2"], p["w_in"],
      tables["rcos"], tables["rsin"], tables["acos"], tables["asin_lo"], tables["asin_hi"],
      p["qg"], p["kg"])


def _retention_stream(q_ref, k_ref, v_ref, g_ref, beta_ref, inner_ref, qdec_ref, kdec_ref, cdec_ref,
                      o_ref):
    c = RET_KERNEL_CHUNK
    n_chunks = q_ref.shape[1] // c
    rows = lambda i: slice(i * c, (i + 1) * c)
    live = {"state": None}

    def begin(i):
        live["scores", i] = lax.dot_general(q_ref[0, rows(i), :], k_ref[0, rows(i), :], NT,
                                            preferred_element_type=F32) * inner_ref[0]
        if i < n_chunks - 1:
            kd = (k_ref[0, rows(i), :].astype(F32) * kdec_ref[0]).astype(BF16)
            live["increment", i] = lax.dot_general(kd, v_ref[0, rows(i), :], TN,
                                                   preferred_element_type=F32)

    def finish(i):
        o = jnp.dot(live.pop(("scores", i)).astype(BF16), v_ref[0, rows(i), :],
                    preferred_element_type=F32)
        state = live["state"]
        if state is not None:
            o = o + jnp.dot(q_ref[0, rows(i), :], state.astype(BF16),
                            preferred_element_type=F32) * qdec_ref[0]
        if i < n_chunks - 1:
            inc = live.pop(("increment", i))
            live["state"] = inc if state is None else state * cdec_ref[0] + inc
        g = g_ref[0, rows(i), :]
        o_ref[0, rows(i), :] = (_rms_rows(o) * (g * jax.nn.sigmoid(g)) * beta_ref[...]).astype(o_ref.dtype)

    return begin, finish


ONES_ROWS = 16
STAT_ROWS = 8
KMEAN_ROWS = 16
SCORE_LOOKAHEAD = 6
GROUPS_PER_STEP = 2
RET_CHAIN_STRIDE = 8
RET_FINISH_AFTER = 3


def _mixer_kernel(rq_ref, rk_ref, rv_ref, rg_ref, rbeta_ref, inner_ref, qdec_ref, kdec_ref, cdec_ref,
                  q_ref, k_ref, v_ref, beta_ref, ret_ref, o_ref, vt_ref):
    bs = MOBA_BLOCK
    seq = k_ref.shape[1]
    nb = seq // bs
    dh = ATT_HEAD_DIM
    L = V7X_LANES
    heads = range(HEADS_PER_LANE_TILE)
    groups = range(GROUPS_PER_STEP)
    lanes = lambda g: slice(g * L, (g + 1) * L)

    ret_hooks = [_retention_stream(rq_ref.at[:, :, lanes(g)], rk_ref.at[:, :, lanes(g)],
                                   rv_ref.at[:, :, lanes(g)], rg_ref.at[:, :, lanes(g)],
                                   rbeta_ref.at[:, lanes(g)], inner_ref.at[g:g + 1],
                                   qdec_ref.at[g:g + 1], kdec_ref.at[g:g + 1], cdec_ref.at[g:g + 1],
                                   ret_ref.at[:, :, lanes(g)]) for g in groups]

    km_hi, km_lo = {}, {}
    for g in groups:
        vt = jnp.transpose(v_ref[0, :, lanes(g)].astype(F32))
        for h in heads:
            for n in range(nb):
                vt_ref[g, h, n, :dh, :] = vt[h * dh:(h + 1) * dh, n * bs:(n + 1) * bs].astype(BF16)
                vt_ref[g, h, n, dh:, :] = jnp.ones((ONES_ROWS, bs), BF16)
        kmean = jnp.mean(k_ref[0, :, lanes(g)].astype(F32).reshape(nb, bs, L), axis=1)
        kmean = jnp.concatenate([kmean, jnp.zeros((KMEAN_ROWS - nb, L), F32)], axis=0)
        km_hi[g], km_lo[g] = _split_bf16(kmean)

    lane = lax.broadcasted_iota(jnp.int32, (bs, L), 1)
    head_lanes = [(lane >= h * dh) & (lane < (h + 1) * dh) for h in heads]
    row_id = lax.broadcasted_iota(jnp.int32, (KMEAN_ROWS, bs), 0)
    causal_t = (lax.broadcasted_iota(jnp.int32, (bs, bs), 0)
                <= lax.broadcasted_iota(jnp.int32, (bs, bs), 1))
    stat = lambda t: jnp.broadcast_to(t, (STAT_ROWS, bs))
    wide = lambda t: jnp.broadcast_to(t[:1], (dh, bs))

    masked_q = {}

    def query(g, j, h):
        if (g, j, h) not in masked_q:
            q = q_ref[0, j * bs:(j + 1) * bs, lanes(g)]
            masked_q[g, j, h] = jnp.where(head_lanes[h], q, jnp.zeros_like(q))
        return masked_q[g, j, h]

    selected = {}

    def selection(g, j, h):
        if (g, j, h) not in selected:
            sel = [None] * j
            if j > MOBA_TOPK:
                gate = (lax.dot_general(km_hi[g], query(g, j, h), NT, preferred_element_type=F32)
                        + lax.dot_general(km_lo[g], query(g, j, h), NT, preferred_element_type=F32))
                gate = jnp.where(row_id < j, gate, -jnp.inf)
                rank = jnp.zeros(gate.shape, jnp.int32)
                for m in range(j):
                    other = gate[m:m + 1, :]
                    ahead = (other > gate) | ((other == gate) & (m < row_id))
                    rank = rank + ahead.astype(jnp.int32)
                keep = (rank < MOBA_TOPK).astype(F32)
                sel = [stat(keep[n:n + 1, :]) > 0.5 for n in range(j)]
            selected[g, j, h] = sel
        return selected[g, j, h]

    def combine(g, j, h, pv):
        sel = selection(g, j, h)
        m_all = pv[j][0]
        for n in range(j):
            m_n = pv[n][0]
            m_all = jnp.maximum(m_all, m_n if sel[n] is None else jnp.where(sel[n], m_n, -jnp.inf))
        l_sum = jnp.zeros((STAT_ROWS, bs), F32)
        acc = jnp.zeros((dh, bs), F32)
        for n in range(j + 1):
            m_n, ov = pv[n]
            w = jnp.exp2(m_n - m_all)
            if n < j and sel[n] is not None:
                w = jnp.where(sel[n], w, 0.0)
            l_sum = l_sum + w * ov[dh:dh + STAT_ROWS, :]
            acc = acc + wide(w) * ov[:dh, :]
        res = acc * wide(1.0 / l_sum)
        ms = jnp.mean(res * res, axis=0, keepdims=True)
        return res * lax.rsqrt(ms + NORM_EPS)

    group_work = [(j, n, h) for j in reversed(range(nb)) for n in range(j + 1) for h in heads]
    work = [(g,) + w for g in groups for w in group_work]
    assert RET_CHAIN_STRIDE * (nb - 1) + RET_FINISH_AFTER < len(group_work)

    def retention_hook(kind, chain):
        g, since = divmod(chain, len(group_work))
        since -= kind * RET_FINISH_AFTER
        if since >= 0 and since % RET_CHAIN_STRIDE == 0 and since // RET_CHAIN_STRIDE < nb:
            ret_hooks[g][kind](since // RET_CHAIN_STRIDE)

    scores, pv, normed = {}, {}, {}
    for step in range(-SCORE_LOOKAHEAD, len(work)):
        if step + SCORE_LOOKAHEAD < len(work):
            g, j, n, h = work[step + SCORE_LOOKAHEAD]
            scores[g, j, n, h] = lax.dot_general(k_ref[0, n * bs:(n + 1) * bs, lanes(g)],
                                                 query(g, j, h), NT,
                                                 preferred_element_type=F32)
        if step < 0:
            continue
        g, j, n, h = work[step]
        retention_hook(0, step)
        if n == 0:
            selection(g, j, h)
        st = scores.pop((g, j, n, h))
        if n == j:
            st = jnp.where(causal_t, st, -jnp.inf)
        mt = jnp.max(st, axis=0, keepdims=True)
        pt = jnp.exp2(st - mt).astype(BF16)
        pv.setdefault((g, j, h), {})[n] = (stat(mt), jnp.dot(vt_ref[g, h, n], pt,
                                                             preferred_element_type=F32))
        retention_hook(1, step)
        if n == j:
            normed[g, j, h] = combine(g, j, h, pv.pop((g, j, h)))
            if h == heads[-1]:
                att = jnp.transpose(jnp.concatenate([normed.pop((g, j, hh)) for hh in heads], axis=0))
                o_ref[0, j * bs:(j + 1) * bs, lanes(g)] = (
                    att * beta_ref[:, lanes(g)]).astype(o_ref.dtype)


def _mixers(rq, rk, rv, rg, ret_beta, aq, ak, av, att_beta, tables):
    b, s, _ = aq.shape
    L = V7X_LANES
    nb = s // MOBA_BLOCK
    c = RET_KERNEL_CHUNK
    assert nb <= KMEAN_ROWS
    assert RET_HEADS == ATT_LANE_TILES and RET_HEAD_DIM == L and c == MOBA_BLOCK
    gps = GROUPS_PER_STEP
    assert ATT_LANE_TILES % gps == 0
    cols = lambda: pl.BlockSpec((1, s, gps * L), lambda bi, t: (bi, 0, t))
    gain = lambda: pl.BlockSpec((1, gps * L), lambda bi, t: (0, t))
    per_head = lambda r, w: pl.BlockSpec((gps, r, w), lambda bi, t: (t, 0, 0))
    return pl.pallas_call(
        _mixer_kernel,
        grid=(b, ATT_LANE_TILES // gps),
        in_specs=[cols(), cols(), cols(), cols(), gain(),
                  per_head(c, c), per_head(c, L), per_head(c, L), per_head(1, L),
                  cols(), cols(), cols(), gain()],
        out_specs=[cols(), cols()],
        out_shape=[jax.ShapeDtypeStruct((b, s, RET_WIDTH), BF16),
                   jax.ShapeDtypeStruct((b, s, ATT_WIDTH), BF16)],
        scratch_shapes=[pltpu.VMEM((gps, HEADS_PER_LANE_TILE, nb, ATT_HEAD_DIM + ONES_ROWS, MOBA_BLOCK),
                                   BF16)],
        compiler_params=_params(),
        name="mixers",
    )(rq, rk, rv, rg, ret_beta, tables["inner_decay"], tables["q_decay"], tables["k_decay"],
      tables["chunk_decay"], aq, ak, av, att_beta)


def _outproj_ffn_kernel(x1_ref, ret_ref, att_ref, wo_ref, n_ref, wg_ref, wu_ref, wd_ref,
                        o_ref):
    half = x1_ref.shape[0] // 2
    halves = (slice(0, half), slice(half, 2 * half))
    x2 = [x1_ref[r, :]
          + jnp.dot(jnp.concatenate([ret_ref[r, :], att_ref[r, :]], axis=1), wo_ref[...],
                    preferred_element_type=F32) for r in halves]
    h = [(_rms_rows(t) * n_ref[...]).astype(BF16) for t in x2]
    a = [_swiglu_activation(t, wg_ref, wu_ref) for t in h]
    for r, t, act in zip(halves, x2, a):
        o_ref[r, :] = t + 0.5 * jnp.dot(act, wd_ref[...], preferred_element_type=F32)


def _outproj_ffn(x1, ret, att, seq, p):
    tokens = x1.shape[0]
    tm = _tiles(seq)["row_tile"]
    row = lambda w: pl.BlockSpec((tm, w), lambda i: (i, 0))
    return pl.pallas_call(
        _outproj_ffn_kernel,
        grid=(tokens // tm,),
        in_specs=[row(D_MODEL), row(RET_WIDTH), row(ATT_WIDTH),
                  _resident((RET_WIDTH + ATT_WIDTH, D_MODEL)),
                  _resident((1, D_MODEL)),
                  _resident((D_MODEL, FFN_DIM)), _resident((D_MODEL, FFN_DIM)),
                  _resident((FFN_DIM, D_MODEL))],
        out_specs=row(D_MODEL),
        out_shape=jax.ShapeDtypeStruct((tokens, D_MODEL), F32),
        compiler_params=_params(),
        name="outproj_ffn2",
    )(x1, ret, att, p["wo"], p["n3"], p["wg2"], p["wu2"], p["wd2"])


def _tables(seq):
    pos = np.arange(seq, dtype=np.float64)[:, None]

    def angles(theta, rot_dim):
        half = rot_dim // 2
        inv_freq = theta ** (-np.arange(half, dtype=np.float64) * 2.0 / rot_dim)
        return pos * inv_freq[None, :]

    ang = angles(RET_ROPE_BASE, RET_HEAD_DIM)
    rcos = np.concatenate([np.cos(ang), np.cos(ang)], axis=1)
    rsin = np.concatenate([-np.sin(ang), np.sin(ang)], axis=1)

    ang = angles(ATT_ROPE_BASE, ATT_ROT_DIM)
    half = ATT_ROT_DIM // 2
    pad = np.zeros((seq, ATT_HEAD_DIM - ATT_ROT_DIM))
    zero_half = np.zeros((seq, half))
    head_cos = np.concatenate([np.cos(ang), np.cos(ang), 1.0 + pad], axis=1)
    head_sin_lo = np.concatenate([-np.sin(ang), zero_half, pad], axis=1)
    head_sin_hi = np.concatenate([zero_half, np.sin(ang), pad], axis=1)
    rep = lambda t: np.tile(t, (1, HEADS_PER_LANE_TILE))

    c = RET_KERNEL_CHUNK
    log_g = np.log(1.0 - 2.0 ** (-5.0 - np.arange(RET_HEADS, dtype=np.float64)))
    i = np.arange(c, dtype=np.float64)
    diff = i[:, None] - i[None, :]
    inner = np.where(diff[None] >= 0,
                     np.exp(np.maximum(diff, 0.0)[None] * log_g[:, None, None]), 0.0)
    lanes = lambda t: np.broadcast_to(t[:, :, None], (RET_HEADS, t.shape[1], RET_HEAD_DIM))
    q_decay = lanes(np.exp((i[None, :] + 1.0) * log_g[:, None]))
    k_decay = lanes(np.exp((c - 1.0 - i[None, :]) * log_g[:, None]))
    chunk_decay = lanes(np.exp(c * log_g)[:, None])

    tables = dict(rcos=rcos, rsin=rsin, acos=rep(head_cos), asin_lo=rep(head_sin_lo),
                  asin_hi=rep(head_sin_hi), inner_decay=inner, q_decay=q_decay, k_decay=k_decay,
                  chunk_decay=chunk_decay)
    return {name: jnp.asarray(np.ascontiguousarray(t), dtype=F32) for name, t in tables.items()}


def kernel(x, ffn1_norm_w, ffn1_w_gate, ffn1_w_up, ffn1_w_down, mix_norm_w, w_in, ret_out_beta,
           q_norm_w, k_norm_w, att_out_beta, w_out, ffn2_norm_w, ffn2_w_gate, ffn2_w_up, ffn2_w_down):
    b, s, d = x.shape
    assert d == D_MODEL
    tables = _tables(s)
    depth = ffn1_norm_w.shape[0]
    q_logit_scale = ATT_HEAD_DIM ** -0.5 * math.log2(math.e)
    x2d = x.reshape(b * s, d)
    for l in range(depth):
        p = dict(
            n1=ffn1_norm_w[l][None, :], wg1=ffn1_w_gate[l].astype(BF16), wu1=ffn1_w_up[l].astype(BF16),
            wd1=ffn1_w_down[l].astype(BF16), n2=mix_norm_w[l][None, :], w_in=w_in[l].astype(BF16),
            qg=jnp.tile(q_norm_w[l] * q_logit_scale, ATT_HEADS)[None, :],
            kg=jnp.tile(k_norm_w[l], ATT_HEADS)[None, :],
            wo=w_out[l].astype(BF16),
            n3=ffn2_norm_w[l][None, :], wg2=ffn2_w_gate[l].astype(BF16), wu2=ffn2_w_up[l].astype(BF16),
            wd2=ffn2_w_down[l].astype(BF16))
        x1, rq, rk, rv, rg, aq, ak, av = _ffn_inproj(x2d, s, p, tables)
        seq3 = lambda t: t.reshape(b, s, t.shape[-1])
        ret, att = _mixers(seq3(rq), seq3(rk), seq3(rv), seq3(rg), ret_out_beta[l][None, :],
                           seq3(aq), seq3(ak), seq3(av), att_out_beta[l][None, :], tables)
        x2d = _outproj_ffn(x1, ret.reshape(b * s, RET_WIDTH), att.reshape(b * s, ATT_WIDTH), s, p)
    return x2d.reshape(b, s, d)
```
